```python
import math
import jax, jax.numpy as jnp
from jax import lax
import numpy as np

D_MODEL = 1024
BATCH = 16
SEQ = 256
DEPTH = 4
DEC_BATCH = 2
DEC_SEQ = 4096
PAST_LEN = 256

GRID_W = 64
D_MIX = 2 * D_MODEL
N_MIXERS = 4
D_BRANCH = D_MIX // N_MIXERS
HEAD_DIM = 64
N_HEADS = D_BRANCH // HEAD_DIM
N_DIR = 2
CHUNK = 64
CONV_W = 4
CONV_PAD = ((CONV_W - 1) // 2, CONV_W // 2)
LRU_C = 8.0
LORA_W = 64
LORA_A = 64
ROPE_BASE = 100.0
EPS = 1e-6

P_MLSTM = 5 * D_BRANCH + 4 * N_HEADS
P_LRU = 2 * D_BRANCH
P_RET = 4 * D_BRANCH
RWKV_SHIFT = 3 * D_BRANCH + LORA_W + LORA_A
P_RWKV = RWKV_SHIFT + D_BRANCH
P_IN = P_MLSTM + P_LRU + P_RET + P_RWKV
SPLITS = [P_MLSTM, P_MLSTM + P_LRU, P_MLSTM + P_LRU + P_RET]

kernel_name = 'hybrid_parallel_heads_diffusion_step'


def rmsnorm(x, g):
    x32 = x.astype(jnp.float32)
    y = x32 * lax.rsqrt(jnp.mean(x32 * x32, -1, keepdims=True) + EPS)
    return (y * g.astype(jnp.float32)).astype(x.dtype)


def head_norm(x):
    s = x.shape
    xh = x.reshape(s[:-1] + (N_HEADS, HEAD_DIM))
    xh = xh * lax.rsqrt(jnp.mean(xh * xh, -1, keepdims=True) + EPS)
    return xh.reshape(s)


def to_heads(x):
    b, n, _ = x.shape
    return x.reshape(b, n, N_HEADS, HEAD_DIM).transpose(0, 2, 1, 3).astype(jnp.float32)


def from_heads(x):
    b, h, n, d = x.shape
    return x.transpose(0, 2, 1, 3).reshape(b, n, h * d)


def to_chunks(x):
    b, h, n = x.shape[:3]
    return jnp.moveaxis(x.reshape((b, h, n // CHUNK, CHUNK) + x.shape[3:]), 2, 0)


def from_chunks(x):
    nc, b, h, l = x.shape[:4]
    return jnp.moveaxis(x, 0, 2).reshape((b, h, nc * l) + x.shape[4:])


def flip_t(x):
    return jnp.flip(x, axis=2)


def mlstm_scan(q, k, v, ig, lf, c0, n0, m0):
    tril = jnp.tril(jnp.ones((CHUNK, CHUNK), dtype=bool))

    def step(carry, xs):
        c, n, m = carry
        qc, kc, vc, igc, lfc = xs
        b = jnp.cumsum(lfc, axis=-1)
        dmat = jnp.where(tril, b[..., :, None] - b[..., None, :] + igc[..., None, :], -jnp.inf)
        m_inter = b + m[..., None]
        m_t = jnp.maximum(m_inter, jnp.max(dmat, -1))
        s = jnp.einsum('bhtd,bhsd->bhts', qc, kc) * jnp.exp(dmat - m_t[..., None])
        w_inter = jnp.exp(m_inter - m_t)
        num = jnp.einsum('bhts,bhsv->bhtv', s, vc) + w_inter[..., None] * jnp.einsum('bhtd,bhdv->bhtv', qc, c)
        den = jnp.sum(s, -1) + w_inter * jnp.einsum('bhtd,bhd->bht', qc, n)
        h = num / jnp.maximum(jnp.abs(den), jnp.exp(-m_t))[..., None]
        b_last = b[..., -1]
        w_log = b_last[..., None] - b + igc
        m_new = jnp.maximum(b_last + m, jnp.max(w_log, -1))
        dec = jnp.exp(b_last + m - m_new)
        wk = kc * jnp.exp(w_log - m_new[..., None])[..., None]
        c_new = dec[..., None, None] * c + jnp.einsum('bhsd,bhsv->bhdv', wk, vc)
        n_new = dec[..., None] * n + jnp.sum(wk, axis=2)
        return (c_new, n_new, m_new), h

    xs = (to_chunks(q), to_chunks(k), to_chunks(v), to_chunks(ig), to_chunks(lf))
    (c, n, m), hs = lax.scan(step, (c0, n0, m0), xs)
    return from_chunks(hs), c, n, m


def mlstm_branch(u, gate_b, c0, n0, m0):
    f32 = jnp.float32
    b, n, _ = u.shape
    q, k, v, o, g = (u[..., i * D_BRANCH:(i + 1) * D_BRANCH] for i in range(5))
    gates = u[..., 5 * D_BRANCH:].astype(f32) + gate_b.astype(f32)
    gates = gates.reshape(b, n, 4, N_HEADS).transpose(2, 0, 3, 1)
    qh, kh, vh = to_heads(q), to_heads(k) * HEAD_DIM ** -0.5, to_heads(v)
    hf, cf, nf, mf = mlstm_scan(qh, kh, vh, gates[0], jax.nn.log_sigmoid(gates[1]),
                                c0[:, 0].astype(f32), n0[:, 0].astype(f32), m0[:, 0].astype(f32))
    hb, cb, nb, mb = mlstm_scan(flip_t(qh), flip_t(kh), flip_t(vh), flip_t(gates[2]),
                                flip_t(jax.nn.log_sigmoid(gates[3])),
                                c0[:, 1].astype(f32), n0[:, 1].astype(f32), m0[:, 1].astype(f32))
    h = from_heads(hf + flip_t(hb)) * jax.nn.sigmoid(o.astype(f32))
    y = head_norm(h) * jax.nn.silu(g.astype(f32))
    return (y.astype(u.dtype), jnp.stack([cf, cb], 1), jnp.stack([nf, nb], 1), jnp.stack([mf, mb], 1))


def linear_scan(a, bx, h0):
    def combine(l, r):
        return (l[0] * r[0], r[0] * l[1] + r[1])
    a_cum, b_cum = lax.associative_scan(combine, (a, bx), axis=1)
    return a_cum * h0[:, None] + b_cum


def rglru_branch(u, conv_w, conv_b, gate_w, gate_b, lam, h0):
    f32 = jnp.float32
    b, n, _ = u.shape
    xb, g = u[..., :D_BRANCH], u[..., D_BRANCH:]
    xc = lax.conv_general_dilated(xb, conv_w.astype(xb.dtype)[:, None, :], window_strides=(1,),
                                  padding=[CONV_PAD], dimension_numbers=('NWC', 'WIO', 'NWC'),
                                  feature_group_count=D_BRANCH)
    xc = xc.astype(f32) + conv_b.astype(f32)

    def direction(d, xs, h_init):
        xr = xs.reshape(b, n, N_HEADS, HEAD_DIM)
        gr = jnp.einsum('bnhi,hij->bnhj', xr, gate_w[d, 0].astype(f32)).reshape(b, n, D_BRANCH) + gate_b[d, 0].astype(f32)
        gi = jnp.einsum('bnhi,hij->bnhj', xr, gate_w[d, 1].astype(f32)).reshape(b, n, D_BRANCH) + gate_b[d, 1].astype(f32)
        log_a = -LRU_C * jax.nn.sigmoid(gr) * jax.nn.softplus(-lam[d].astype(f32))
        a = jnp.exp(log_a)
        beta = jnp.sqrt(-jnp.expm1(2.0 * log_a))
        return linear_scan(a, beta * jax.nn.sigmoid(gi) * xs, h_init)

    hf = direction(0, xc, h0[:, 0].astype(f32))
    hb = direction(1, jnp.flip(xc, 1), h0[:, 1].astype(f32))
    y = (hf + jnp.flip(hb, 1)) * jax.nn.silu(g.astype(f32))
    return y.astype(u.dtype), jnp.stack([hf[:, -1], hb[:, -1]], 1)


def apply_rope(x, rope):
    cos, sin = rope
    x1, x2 = x[..., :HEAD_DIM // 2], x[..., HEAD_DIM // 2:]
    return jnp.concatenate([x1 * cos - x2 * sin, x1 * sin + x2 * cos], -1)


def retention_scan(q, k, v, log_g, r0):
    pos = jnp.arange(CHUNK, dtype=jnp.float32)
    diff = pos[:, None] - pos[None, :]
    dmat = jnp.where(diff >= 0, jnp.exp(log_g[:, None, None] * jnp.maximum(diff, 0.0)), 0.0)
    xi = jnp.exp(log_g[:, None] * (pos + 1.0))[..., None]
    wk = jnp.exp(log_g[:, None] * (CHUNK - 1.0 - pos))[..., None]
    dec = jnp.exp(log_g * CHUNK)[:, None, None]

    def step(r, xs):
        qc, kc, vc = xs
        s = jnp.einsum('bhtd,bhsd->bhts', qc, kc) * dmat
        o = jnp.einsum('bhts,bhsv->bhtv', s, vc) + xi * jnp.einsum('bhtd,bhdv->bhtv', qc, r)
        r_new = dec * r + jnp.einsum('bhsd,bhsv->bhdv', kc * wk, vc)
        return r_new, o

    r, outs = lax.scan(step, r0, (to_chunks(q), to_chunks(k), to_chunks(v)))
    return from_chunks(outs), r


def retention_branch(u, theta, r0, rope):
    f32 = jnp.float32
    q, k, v, g = (u[..., i * D_BRANCH:(i + 1) * D_BRANCH] for i in range(4))
    qh, kh, vh = to_heads(q), to_heads(k) * HEAD_DIM ** -0.5, to_heads(v)
    if rope is not None:
        qh, kh = apply_rope(qh, rope), apply_rope(kh, rope)
    log_g = jax.nn.log_sigmoid(theta.astype(f32))
    yf, rf = retention_scan(qh, kh, vh, log_g[0], r0[:, 0].astype(f32))
    yb, rb = retention_scan(flip_t(qh), flip_t(kh), flip_t(vh), log_g[1], r0[:, 1].astype(f32))
    y = head_norm(from_heads(yf + flip_t(yb))) * jax.nn.silu(g.astype(f32))
    return y.astype(u.dtype), jnp.stack([rf, rb], 1)


def shift_1d(s):
    half = s.shape[-1] // 2
    prev = jnp.pad(s[:, :-1, :half], ((0, 0), (1, 0), (0, 0)))
    nxt = jnp.pad(s[:, 1:, half:], ((0, 0), (0, 1), (0, 0)))
    return jnp.concatenate([prev, nxt], -1)


def shift_grid(s, rows):
    b, n, c = s.shape
    qc = c // 4
    x = s.reshape(b, rows, GRID_W, c)
    left = jnp.pad(x[:, :, :-1, :qc], ((0, 0), (0, 0), (1, 0), (0, 0)))
    right = jnp.pad(x[:, :, 1:, qc:2 * qc], ((0, 0), (0, 0), (0, 1), (0, 0)))
    up = jnp.pad(x[:, :-1, :, 2 * qc:3 * qc], ((0, 0), (1, 0), (0, 0), (0, 0)))
    down = jnp.pad(x[:, 1:, :, 3 * qc:], ((0, 0), (0, 1), (0, 0), (0, 0)))
    return jnp.concatenate([left, right, up, down], -1).reshape(b, n, c)


def rwkv_scan(r, w, k, v, kk, bb, s0):
    bsz, n, _ = r.shape

    def seq(t):
        return jnp.moveaxis(t.reshape(bsz, n, N_HEADS, HEAD_DIM), 1, 0)

    def step(s, xs):
        rt, wt, kt, vt, kkt, bt = xs
        sa = jnp.einsum('bhij,bhj->bhi', s, -kkt)
        s = s * wt[:, :, None, :] + sa[..., None] * bt[:, :, None, :] + vt[..., None] * kt[:, :, None, :]
        return s, jnp.einsum('bhij,bhj->bhi', s, rt)

    s, ys = lax.scan(step, s0, (seq(r), seq(w), seq(k), seq(v), seq(kk), seq(bb)))
    return jnp.moveaxis(ys, 0, 1).reshape(bsz, n, D_BRANCH), s


def rwkv_branch(u, mu, w0, w2, a0, a2, k_k, k_a, r_k, s0, rows):
    f32 = jnp.float32
    b, n, _ = u.shape
    s, g = u[..., :RWKV_SHIFT], u[..., RWKV_SHIFT:]
    sh = shift_1d(s) if rows is None else shift_grid(s, rows)
    s = (s + mu * (sh - s)).astype(f32)
    r = s[..., :D_BRANCH]
    k = s[..., D_BRANCH:2 * D_BRANCH]
    v = s[..., 2 * D_BRANCH:3 * D_BRANCH]
    lw = jnp.tanh(s[..., 3 * D_BRANCH:3 * D_BRANCH + LORA_W])
    la = s[..., 3 * D_BRANCH + LORA_W:]
    kkh = (k * k_k.astype(f32)).reshape(b, n, N_HEADS, HEAD_DIM)
    kk = (kkh / jnp.maximum(jnp.sqrt(jnp.sum(kkh * kkh, -1, keepdims=True)), 1e-12)).reshape(b, n, D_BRANCH)
    outs, finals = [], []
    for d in range(N_DIR):
        w_log = -jax.nn.softplus(-(w0[d].astype(f32) + lw @ w2[d].astype(f32))) - 0.5
        decay = jnp.exp(-jnp.exp(w_log))
        a = jax.nn.sigmoid(a0[d].astype(f32) + la @ a2[d].astype(f32))
        kt = k * (1.0 + (a - 1.0) * k_a.astype(f32))
        seqs = (r, decay, kt, v, kk, kk * a)
        if d == 1:
            seqs = tuple(jnp.flip(t, 1) for t in seqs)
        y, sf = rwkv_scan(*seqs, s0[:, d].astype(f32))
        outs.append(y if d == 0 else jnp.flip(y, 1))
        finals.append(sf)
    bonus = jnp.sum((r * k * r_k.astype(f32)).reshape(b, n, N_HEADS, HEAD_DIM), -1, keepdims=True) * v.reshape(b, n, N_HEADS, HEAD_DIM)
    y = (head_norm(outs[0] + outs[1]) + bonus.reshape(b, n, D_BRANCH)) * jax.nn.silu(g.astype(f32))
    return y.astype(u.dtype), jnp.stack(finals, 1)


def trunk_layer(x, mod, p, states, rows, rope):
    f32 = jnp.float32
    shift, scale, gate = jnp.split(mod[:, None, :].astype(f32), 3, axis=-1)
    h = (rmsnorm(x, p['norm_g']).astype(f32) * (1.0 + scale) + shift).astype(x.dtype)
    u = h @ p['w_in'].astype(x.dtype)
    u_a, u_b, u_c, u_d = jnp.split(u, SPLITS, axis=-1)
    c_m, n_m, m_m, h_l, r_r, s_w = states
    y_a, c_new, n_new, m_new = mlstm_branch(u_a, p['mlstm_gate_b'], c_m, n_m, m_m)
    y_b, h_new = rglru_branch(u_b, p['lru_conv_w'], p['lru_conv_b'], p['lru_gate_w'], p['lru_gate_b'],
                              p['lru_lambda'], h_l)
    y_c, r_new = retention_branch(u_c, p['ret_theta'], r_r, rope)
    y_d, s_new = rwkv_branch(u_d, p['rwkv_mu'], p['rwkv_w0'], p['rwkv_w2'], p['rwkv_a0'], p['rwkv_a2'],
                             p['rwkv_kk'], p['rwkv_ka'], p['rwkv_rk'], s_w, rows)
    y = jnp.concatenate([y_a, y_b, y_c, y_d], -1) @ p['w_out'].astype(x.dtype)
    x = (x.astype(f32) + gate * y.astype(f32)).astype(x.dtype)
    return x, (c_new, n_new, m_new, h_new, r_new, s_new)


def setup_inputs(seed: int = 0) -> dict:
    key = jax.random.key(seed)
    ks = iter(jax.random.split(key, 40))
    f32 = jnp.float32

    def nrm(shape, s):
        return s * jax.random.normal(next(ks), shape, f32)

    H, HD, DB = N_HEADS, HEAD_DIM, D_BRANCH
    x_prompt = nrm((BATCH, SEQ, D_MODEL), 1.0)
    x_sample = nrm((DEC_BATCH, DEC_SEQ, D_MODEL), 1.0)
    c = nrm((DEC_BATCH, D_MODEL), 1.0)
    state_mlstm_c = nrm((DEC_BATCH, DEPTH, N_DIR, H, HD, HD), 0.1)
    state_mlstm_n = nrm((DEC_BATCH, DEPTH, N_DIR, H, HD), 0.1)
    state_mlstm_m = nrm((DEC_BATCH, DEPTH, N_DIR, H), 1.0)
    state_lru_h = nrm((DEC_BATCH, DEPTH, N_DIR, DB), 0.5)
    state_ret_r = nrm((DEC_BATCH, DEPTH, N_DIR, H, HD, HD), 0.3)
    state_rwkv_s = nrm((DEC_BATCH, DEPTH, N_DIR, H, HD, HD), 0.3)
    c_ctx = nrm((D_MODEL,), 1.0)
    norm_g = 1.0 + nrm((DEPTH, D_MODEL), 0.02)
    w_mod = nrm((DEPTH, D_MODEL, 3 * D_MODEL), D_MODEL ** -0.5)
    b_mod = nrm((DEPTH, 3 * D_MODEL), 0.02)
    w_in = nrm((DEPTH, D_MODEL, P_IN), D_MODEL ** -0.5)
    w_out = nrm((DEPTH, D_MIX, D_MODEL), D_MIX ** -0.5)
    fb = jnp.linspace(3.0, 6.0, H, dtype=f32)
    zb = jnp.zeros((H,), f32)
    mlstm_gate_b = jnp.concatenate([zb, fb, zb, fb])[None] + nrm((DEPTH, 4 * H), 0.1)
    lru_conv_w = nrm((DEPTH, CONV_W, DB), CONV_W ** -0.5)
    lru_conv_b = nrm((DEPTH, DB), 0.02)
    lru_gate_w = nrm((DEPTH, N_DIR, 2, H, HD, HD), HD ** -0.5)
    lru_gate_b = nrm((DEPTH, N_DIR, 2, DB), 0.02)
    a_target = jax.random.uniform(next(ks), (DEPTH, N_DIR, DB), f32, minval=0.9, maxval=0.999)
    sig = a_target ** (1.0 / LRU_C)
    lru_lambda = jnp.log(sig) - jnp.log1p(-sig)
    gamma = 1.0 - 2.0 ** (-5.0 - jnp.arange(H, dtype=f32))
    ret_theta = (jnp.log(gamma) - jnp.log1p(-gamma)) + nrm((DEPTH, N_DIR, H), 0.01)
    rwkv_mu = jax.random.uniform(next(ks), (DEPTH, RWKV_SHIFT), f32)
    rwkv_w0 = jnp.linspace(-6.0, -1.0, DB, dtype=f32) + nrm((DEPTH, N_DIR, DB), 0.1)
    rwkv_w2 = nrm((DEPTH, N_DIR, LORA_W, DB), 0.1 * LORA_W ** -0.5)
    rwkv_a0 = nrm((DEPTH, N_DIR, DB), 0.1)
    rwkv_a2 = nrm((DEPTH, N_DIR, LORA_A, DB), 0.5 * LORA_A ** -0.5)
    rwkv_kk = 0.85 + nrm((DEPTH, DB), 0.02)
    rwkv_ka = 1.0 + nrm((DEPTH, DB), 0.02)
    rwkv_rk = nrm((DEPTH, DB), 0.1)
    final_g = 1.0 + nrm((D_MODEL,), 0.02)
    return {'x_prompt': x_prompt, 'x_sample': x_sample, 'c': c,
            'state_mlstm_c': state_mlstm_c, 'state_mlstm_n': state_mlstm_n, 'state_mlstm_m': state_mlstm_m,
            'state_lru_h': state_lru_h, 'state_ret_r': state_ret_r, 'state_rwkv_s': state_rwkv_s,
            'c_ctx': c_ctx, 'norm_g': norm_g, 'w_mod': w_mod, 'b_mod': b_mod, 'w_in': w_in, 'w_out': w_out,
            'mlstm_gate_b': mlstm_gate_b, 'lru_conv_w': lru_conv_w, 'lru_conv_b': lru_conv_b,
            'lru_gate_w': lru_gate_w, 'lru_gate_b': lru_gate_b, 'lru_lambda': lru_lambda,
            'ret_theta': ret_theta, 'rwkv_mu': rwkv_mu, 'rwkv_w0': rwkv_w0, 'rwkv_w2': rwkv_w2,
            'rwkv_a0': rwkv_a0, 'rwkv_a2': rwkv_a2, 'rwkv_kk': rwkv_kk, 'rwkv_ka': rwkv_ka,
            'rwkv_rk': rwkv_rk, 'final_g': final_g}


def reference(x_prompt, x_sample, c, state_mlstm_c, state_mlstm_n, state_mlstm_m, state_lru_h, state_ret_r,
              state_rwkv_s, c_ctx, norm_g, w_mod, b_mod, w_in, w_out, mlstm_gate_b, lru_conv_w, lru_conv_b,
              lru_gate_w, lru_gate_b, lru_lambda, ret_theta, rwkv_mu, rwkv_w0, rwkv_w2, rwkv_a0, rwkv_a2,
              rwkv_kk, rwkv_ka, rwkv_rk, final_g):
    f32 = jnp.float32

    def layer_params(l):
        return dict(norm_g=norm_g[l], w_in=w_in[l], w_out=w_out[l], mlstm_gate_b=mlstm_gate_b[l],
                    lru_conv_w=lru_conv_w[l], lru_conv_b=lru_conv_b[l], lru_gate_w=lru_gate_w[l],
                    lru_gate_b=lru_gate_b[l], lru_lambda=lru_lambda[l], ret_theta=ret_theta[l],
                    rwkv_mu=rwkv_mu[l], rwkv_w0=rwkv_w0[l], rwkv_w2=rwkv_w2[l], rwkv_a0=rwkv_a0[l],
                    rwkv_a2=rwkv_a2[l], rwkv_kk=rwkv_kk[l], rwkv_ka=rwkv_ka[l], rwkv_rk=rwkv_rk[l])

    bp = x_prompt.shape[0]
    zero_states = (jnp.zeros((bp, N_DIR, N_HEADS, HEAD_DIM, HEAD_DIM), f32),
                   jnp.zeros((bp, N_DIR, N_HEADS, HEAD_DIM), f32),
                   jnp.zeros((bp, N_DIR, N_HEADS), f32),
                   jnp.zeros((bp, N_DIR, D_BRANCH), f32),
                   jnp.zeros((bp, N_DIR, N_HEADS, HEAD_DIM, HEAD_DIM), f32),
                   jnp.zeros((bp, N_DIR, N_HEADS, HEAD_DIM, HEAD_DIM), f32))
    sc_ctx = jax.nn.silu(c_ctx.astype(f32))
    xp = x_prompt
    per_layer = []
    for l in range(DEPTH):
        mod = (sc_ctx @ w_mod[l].astype(f32) + b_mod[l].astype(f32))[None]
        xp, st = trunk_layer(xp, mod, layer_params(l), zero_states, None, None)
        per_layer.append(st)
    y_prompt = rmsnorm(xp, final_g)
    new_mlstm_c = jnp.stack([st[0] for st in per_layer], 1)
    new_mlstm_n = jnp.stack([st[1] for st in per_layer], 1)
    new_mlstm_m = jnp.stack([st[2] for st in per_layer], 1)
    new_lru_h = jnp.stack([st[3] for st in per_layer], 1)
    new_ret_r = jnp.stack([st[4] for st in per_layer], 1)
    new_rwkv_s = jnp.stack([st[5] for st in per_layer], 1)

    n_lat = x_sample.shape[1]
    rows = n_lat // GRID_W
    row_idx = jnp.broadcast_to(jnp.arange(rows, dtype=f32)[:, None], (rows, GRID_W)).reshape(-1)
    col_idx = jnp.broadcast_to(jnp.arange(GRID_W, dtype=f32)[None, :], (rows, GRID_W)).reshape(-1)
    n_freq = HEAD_DIM // 4
    freqs = ROPE_BASE ** (-jnp.arange(n_freq, dtype=f32) / n_freq)
    ang = jnp.concatenate([row_idx[:, None] * freqs, col_idx[:, None] * freqs], -1)
    rope = (jnp.cos(ang), jnp.sin(ang))
    sc = jax.nn.silu(c.astype(f32))
    xs = x_sample
    for l in range(DEPTH):
        mod = sc @ w_mod[l].astype(f32) + b_mod[l].astype(f32)
        st = (state_mlstm_c[:, l], state_mlstm_n[:, l], state_mlstm_m[:, l], state_lru_h[:, l],
              state_ret_r[:, l], state_rwkv_s[:, l])
        xs, _ = trunk_layer(xs, mod, layer_params(l), st, rows, rope)
    y_sample = rmsnorm(xs, final_g)
    return (y_prompt, y_sample, new_mlstm_c, new_mlstm_n, new_mlstm_m, new_lru_h, new_ret_r, new_rwkv_s)
```

```python
import functools

import jax
import jax.numpy as jnp
from jax import lax
from jax.experimental import pallas as pl
from jax.experimental.pallas import tpu as pltpu

F32 = jnp.float32
BF16 = jnp.bfloat16
HIGHEST = lax.Precision.HIGHEST

D_MODEL = 1024
DEPTH = 4
GRID_W = 64
D_BRANCH = 512
HEAD_DIM = 64
N_HEADS = 8
CHUNK = 64
CONV_W = 4
LRU_C = 8.0
LORA = 64
ROPE_BASE = 100.0
EPS = 1e-6
RWKV_SHIFT = 3 * D_BRANCH + 2 * LORA

LANES = 128
TOKEN_BLOCK = 256
VMEM_LIMIT = 48 * 1024 * 1024
NEG = -1e30


def _sigmoid(x):
    return 1.0 / (1.0 + jnp.exp(-x))


def _silu(x):
    return x * _sigmoid(x)


def _softplus(x):
    return jnp.maximum(x, 0.0) + jnp.log(1.0 + jnp.exp(-jnp.abs(x)))


def _log_sigmoid(x):
    return -_softplus(-x)


def _dot(a, b):
    return jnp.dot(a.astype(BF16), b.astype(BF16), preferred_element_type=F32)


def _dot_nt(a, b):
    return lax.dot_general(a.astype(BF16), b.astype(BF16), (((1,), (1,)), ((), ())),
                           preferred_element_type=F32)


def _dot_tn(a, b):
    return lax.dot_general(a.astype(BF16), b.astype(BF16), (((0,), (0,)), ((), ())),
                           preferred_element_type=F32)


def _dot_hi(a, b):
    return jnp.dot(a, b, preferred_element_type=F32, precision=HIGHEST)


def _dot_nt_hi(a, b):
    return lax.dot_general(a, b, (((1,), (1,)), ((), ())), preferred_element_type=F32,
                           precision=HIGHEST)


def _head_sum(x, ones_bd):
    hi = x.astype(BF16)
    lo = (x - hi.astype(F32)).astype(BF16)
    return (jnp.dot(hi, ones_bd, preferred_element_type=F32)
            + jnp.dot(lo, ones_bd, preferred_element_type=F32))


def _head_norm(x, ones_bd):
    return x * lax.rsqrt(_head_sum(x * x, ones_bd) * (1.0 / HEAD_DIM) + EPS)


def _tri_masks(n, reverse):
    row = lax.broadcasted_iota(jnp.int32, (n, n), 0)
    col = lax.broadcasted_iota(jnp.int32, (n, n), 1)
    incl = (col >= row) if reverse else (col <= row)
    strict = (col > row) if reverse else (col < row)
    return incl, strict, row, col


def _eye(n):
    row = lax.broadcasted_iota(jnp.int32, (n, n), 0)
    col = lax.broadcasted_iota(jnp.int32, (n, n), 1)
    return jnp.where(row == col, 1.0, 0.0).astype(F32)


def _params(*sem):
    return pltpu.CompilerParams(dimension_semantics=sem, vmem_limit_bytes=VMEM_LIMIT)


def _blk_index(reverse, nblk):
    if reverse:
        return lambda b, i: (b, nblk - 1 - i, 0)
    return lambda b, i: (b, i, 0)


def _state_spec(shape):
    nd = len(shape)
    return pl.BlockSpec((1,) + tuple(shape[1:]), lambda b, i: (b,) + (0,) * (nd - 1))


def _const_spec(shape):
    nd = len(shape)
    return pl.BlockSpec(tuple(shape), lambda *_: (0,) * nd)


def _mod_kernel(c_ref, w_ref, b_ref, o_ref):
    sc = _silu(c_ref[...])
    o_ref[0] = _dot(sc, w_ref[0]) + b_ref[0]


def _modulation(cvec, w_mod, b_mod):
    nt = 3
    return pl.pallas_call(
        _mod_kernel,
        out_shape=jax.ShapeDtypeStruct((DEPTH, 8, 3 * D_MODEL), F32),
        grid=(DEPTH, nt),
        in_specs=[pl.BlockSpec((8, D_MODEL), lambda l, j: (0, 0)),
                  pl.BlockSpec((1, D_MODEL, D_MODEL), lambda l, j: (l, 0, j)),
                  pl.BlockSpec((1, 1, D_MODEL), lambda l, j: (l, 0, j))],
        out_specs=pl.BlockSpec((1, 8, D_MODEL), lambda l, j: (l, 0, j)),
        compiler_params=_params("arbitrary", "arbitrary"),
        name="modulation",
    )(cvec, w_mod, b_mod.reshape(DEPTH, 1, 3 * D_MODEL))


def _inproj_kernel(x_ref, g_ref, mod_ref, w_ref, *o_refs, widths):
    x = x_ref[0]
    y = x * lax.rsqrt(jnp.mean(x * x, axis=-1, keepdims=True) + EPS) * g_ref[...]
    h = y * (1.0 + mod_ref[0, 1:2, :]) + mod_ref[0, 0:1, :]
    u = _dot(h, w_ref[...])
    off = 0
    for o_ref, w in zip(o_refs, widths):
        o_ref[0] = u[:, off:off + w]
        off += w


def _inproj(x, g, mod, w, widths):
    b, n, _ = x.shape
    p = w.shape[1]
    tm = TOKEN_BLOCK
    return pl.pallas_call(
        functools.partial(_inproj_kernel, widths=widths),
        out_shape=[jax.ShapeDtypeStruct((b, n, wd), F32) for wd in widths],
        grid=(b, n // tm),
        in_specs=[pl.BlockSpec((1, tm, D_MODEL), lambda i, j: (i, j, 0)),
                  _const_spec((1, D_MODEL)),
                  pl.BlockSpec((1, 3, D_MODEL), lambda i, j: (i, 0, 0)),
                  _const_spec((D_MODEL, p))],
        out_specs=[pl.BlockSpec((1, tm, wd), lambda i, j: (i, j, 0)) for wd in widths],
        compiler_params=_params("parallel", "parallel"),
        name="inproj",
    )(x, g, mod, w)


def _mlstm_kernel(u_ref, gb_ref, c0_ref, n0_ref, m0_ref, y_ref, cf_ref, nf_ref, mf_ref,
                  c_s, n_s, m_s, *, reverse, direction):
    @pl.when(pl.program_id(1) == 0)
    def _():
        c_s[...] = c0_ref[0]
        n_s[...] = n0_ref[0]
        m_s[...] = m0_ref[0]

    L = CHUNK
    nchunk = TOKEN_BLOCK // L
    incl, _, _, _ = _tri_masks(L, reverse)
    incl_f = jnp.where(incl, 1.0, 0.0).astype(F32)
    eye = _eye(LANES)
    ig_off = 2 * N_HEADS * direction
    fg_off = ig_off + N_HEADS
    last = 0 if reverse else L - 1

    def chunk(ci, carry):
        cidx = (nchunk - 1 - ci) if reverse else ci
        r0 = pl.multiple_of(cidx * L, L)
        rows = pl.ds(r0, L)
        gates = u_ref[0, rows, 3 * D_BRANCH:3 * D_BRANCH + LANES] + gb_ref[...]
        lf = _log_sigmoid(gates)
        bmat = _dot_hi(incl_f, lf)
        g_t = _dot_nt_hi(eye, gates)
        b_t = _dot_nt_hi(eye, bmat)
        for h in range(N_HEADS):
            sl = slice(h * HEAD_DIM, (h + 1) * HEAD_DIM)
            qh = u_ref[0, rows, h * HEAD_DIM:(h + 1) * HEAD_DIM]
            kh = u_ref[0, rows, D_BRANCH + h * HEAD_DIM:D_BRANCH + (h + 1) * HEAD_DIM]
            vh = u_ref[0, rows, 2 * D_BRANCH + h * HEAD_DIM:2 * D_BRANCH + (h + 1) * HEAD_DIM]
            ic, fc = ig_off + h, fg_off + h
            b_col = bmat[:, fc:fc + 1]
            ig_col = gates[:, ic:ic + 1]
            b_row = b_t[fc:fc + 1, :]
            ig_row = g_t[ic:ic + 1, :]
            c_prev = c_s[h]
            n_prev = n_s[h]
            m_prev = m_s[:, h:h + 1]
            dmat = jnp.where(incl, b_col - b_row + ig_row, NEG)
            m_inter = b_col + m_prev
            m_t = jnp.maximum(m_inter, jnp.max(dmat, axis=1, keepdims=True))
            s = _dot_nt(qh, kh) * (HEAD_DIM ** -0.5) * jnp.exp(dmat - m_t)
            w_inter = jnp.exp(m_inter - m_t)
            num = _dot(s, vh) + w_inter * _dot(qh, c_prev)
            den = (jnp.sum(s, axis=1, keepdims=True)
                   + w_inter * jnp.sum(qh * n_prev, axis=1, keepdims=True))
            hh = num / jnp.maximum(jnp.abs(den), jnp.exp(-m_t))
            y_ref[0, rows, sl] = hh
            b_last = bmat[last:last + 1, fc:fc + 1]
            w_log = b_last - b_col + ig_col
            m_new = jnp.maximum(b_last + m_prev, jnp.max(w_log, axis=0, keepdims=True))
            dec = jnp.exp(b_last + m_prev - m_new)
            wk = kh * (HEAD_DIM ** -0.5) * jnp.exp(w_log - m_new)
            c_s[h] = dec * c_prev + _dot_tn(wk, vh)
            n_s[h] = dec * n_prev + jnp.sum(wk, axis=0, keepdims=True)
            m_s[:, h:h + 1] = m_new
        return carry

    lax.fori_loop(0, nchunk, chunk, 0)

    @pl.when(pl.program_id(1) == pl.num_programs(1) - 1)
    def _():
        cf_ref[0] = c_s[...]
        nf_ref[0] = n_s[...]
        mf_ref[0] = m_s[...]


def _mlstm_scan(u, gate_b, c0, n0, m0, direction):
    b, n, p = u.shape
    nblk = n // TOKEN_BLOCK
    reverse = direction == 1
    idx = _blk_index(reverse, nblk)
    return pl.pallas_call(
        functools.partial(_mlstm_kernel, reverse=reverse, direction=direction),
        out_shape=[jax.ShapeDtypeStruct((b, n, D_BRANCH), F32),
                   jax.ShapeDtypeStruct(c0.shape, F32),
                   jax.ShapeDtypeStruct(n0.shape, F32),
                   jax.ShapeDtypeStruct(m0.shape, F32)],
        grid=(b, nblk),
        in_specs=[pl.BlockSpec((1, TOKEN_BLOCK, p), idx),
                  _const_spec((1, LANES)),
                  _state_spec(c0.shape), _state_spec(n0.shape), _state_spec(m0.shape)],
        out_specs=[pl.BlockSpec((1, TOKEN_BLOCK, D_BRANCH), idx),
                   _state_spec(c0.shape), _state_spec(n0.shape), _state_spec(m0.shape)],
        scratch_shapes=[pltpu.VMEM(c0.shape[1:], F32), pltpu.VMEM(n0.shape[1:], F32),
                        pltpu.VMEM(m0.shape[1:], F32)],
        compiler_params=_params("parallel", "arbitrary"),
        name="mlstm_scan",
    )(u, gate_b, c0, n0, m0)


def _ret_kernel(*refs, reverse, direction, rope):
    if rope:
        u_ref, th_ref, cos_ref, sin_ref, r0_ref, y_ref, rf_ref, r_s = refs
    else:
        u_ref, th_ref, r0_ref, y_ref, rf_ref, r_s = refs

    @pl.when(pl.program_id(1) == 0)
    def _():
        r_s[...] = r0_ref[0]

    L = CHUNK
    nchunk = TOKEN_BLOCK // L
    incl, _, row, col = _tri_masks(L, reverse)
    diff = jnp.abs(row - col).astype(F32)
    pos_col = lax.broadcasted_iota(jnp.int32, (L, 1), 0).astype(F32)
    p_col = (L - 1.0 - pos_col) if reverse else pos_col
    log_g = _log_sigmoid(th_ref[...])
    lane = lax.broadcasted_iota(jnp.int32, (L, D_BRANCH), 1)
    first_half = (lane & (HEAD_DIM - 1)) < (HEAD_DIM // 2)

    def rot(x, c, s):
        swapped = jnp.where(first_half, pltpu.roll(x, D_BRANCH - HEAD_DIM // 2, axis=1),
                            pltpu.roll(x, HEAD_DIM // 2, axis=1))
        return x * c + swapped * s

    def chunk(ci, carry):
        cidx = (nchunk - 1 - ci) if reverse else ci
        r0 = pl.multiple_of(cidx * L, L)
        rows = pl.ds(r0, L)
        q = u_ref[0, rows, 0:D_BRANCH]
        k = u_ref[0, rows, D_BRANCH:2 * D_BRANCH]
        if rope:
            c = jnp.concatenate([cos_ref[rows, :]] * (D_BRANCH // LANES), axis=1)
            s = jnp.concatenate([sin_ref[rows, :]] * (D_BRANCH // LANES), axis=1)
            q = rot(q, c, s)
            k = rot(k, c, s)
        for h in range(N_HEADS):
            sl = slice(h * HEAD_DIM, (h + 1) * HEAD_DIM)
            lg = log_g[:, N_HEADS * direction + h:N_HEADS * direction + h + 1]
            qh = q[:, sl]
            kh = k[:, sl] * (HEAD_DIM ** -0.5)
            vh = u_ref[0, rows, 2 * D_BRANCH + h * HEAD_DIM:2 * D_BRANCH + (h + 1) * HEAD_DIM]
            r_prev = r_s[h]
            dmat = jnp.where(incl, jnp.exp(lg * diff), 0.0)
            xi = jnp.exp(lg * (p_col + 1.0))
            wk = jnp.exp(lg * (L - 1.0 - p_col))
            dec = jnp.exp(lg * float(L))
            s_mat = _dot_nt(qh, kh) * dmat
            y_ref[0, rows, sl] = _dot(s_mat, vh) + xi * _dot(qh, r_prev)
            r_s[h] = dec * r_prev + _dot_tn(kh * wk, vh)
        return carry

    lax.fori_loop(0, nchunk, chunk, 0)

    @pl.when(pl.program_id(1) == pl.num_programs(1) - 1)
    def _():
        rf_ref[0] = r_s[...]


def _ret_scan(u, theta, r0, direction, rope_tabs):
    b, n, p = u.shape
    nblk = n // TOKEN_BLOCK
    reverse = direction == 1
    idx = _blk_index(reverse, nblk)
    rope = rope_tabs is not None
    in_specs = [pl.BlockSpec((1, TOKEN_BLOCK, p), idx), _const_spec((1, LANES))]
    args = [u, theta]
    if rope:
        tab_idx = (lambda bb, i: (nblk - 1 - i, 0)) if reverse else (lambda bb, i: (i, 0))
        in_specs += [pl.BlockSpec((TOKEN_BLOCK, LANES), tab_idx)] * 2
        args += list(rope_tabs)
    in_specs.append(_state_spec(r0.shape))
    args.append(r0)
    return pl.pallas_call(
        functools.partial(_ret_kernel, reverse=reverse, direction=direction, rope=rope),
        out_shape=[jax.ShapeDtypeStruct((b, n, D_BRANCH), F32), jax.ShapeDtypeStruct(r0.shape, F32)],
        grid=(b, nblk),
        in_specs=in_specs,
        out_specs=[pl.BlockSpec((1, TOKEN_BLOCK, D_BRANCH), idx), _state_spec(r0.shape)],
        scratch_shapes=[pltpu.VMEM(r0.shape[1:], F32)],
        compiler_params=_params("parallel", "arbitrary"),
        name="retention_scan",
    )(*args)


def _lru_kernel(x_ref, xp_ref, xn_ref, cw_ref, cb_ref, gw_ref, gb_ref, lam_ref, h0_ref,
                y_ref, hf_ref, h_s, *, reverse):
    i = pl.program_id(1)
    nblk = pl.num_programs(1)
    blk = (nblk - 1 - i) if reverse else i

    @pl.when(i == 0)
    def _():
        h_s[...] = h0_ref[0]

    tb = TOKEN_BLOCK
    x = x_ref[0]
    row = lax.broadcasted_iota(jnp.int32, (tb, D_BRANCH), 0)
    prev_ok = jnp.where(blk > 0, 1.0, 0.0)
    next_ok = jnp.where(blk < nblk - 1, 1.0, 0.0)
    p_last = xp_ref[0, 7:8, :] * prev_ok
    n_0 = xn_ref[0, 0:1, :] * next_ok
    n_1 = xn_ref[0, 1:2, :] * next_ok
    xm1 = jnp.where(row == 0, p_last, pltpu.roll(x, 1, axis=0))
    xp1 = jnp.where(row == tb - 1, n_0, pltpu.roll(x, tb - 1, axis=0))
    xp2 = jnp.where(row == tb - 1, n_1, jnp.where(row == tb - 2, n_0, pltpu.roll(x, tb - 2, axis=0)))
    xc = (cw_ref[0:1, :] * xm1 + cw_ref[1:2, :] * x + cw_ref[2:3, :] * xp1 + cw_ref[3:4, :] * xp2
          + cb_ref[...])
    gr = _dot(xc, gw_ref[0]) + gb_ref[0:1, :]
    gi = _dot(xc, gw_ref[1]) + gb_ref[1:2, :]
    log_a = -LRU_C * _sigmoid(gr) * _softplus(-lam_ref[...])
    a = jnp.exp(log_a)
    beta = jnp.sqrt(1.0 - jnp.exp(2.0 * log_a))
    bx = beta * _sigmoid(gi) * xc

    sh = 1
    while sh < tb:
        if reverse:
            ok = row < tb - sh
            a_sh = pltpu.roll(a, tb - sh, axis=0)
            b_sh = pltpu.roll(bx, tb - sh, axis=0)
        else:
            ok = row >= sh
            a_sh = pltpu.roll(a, sh, axis=0)
            b_sh = pltpu.roll(bx, sh, axis=0)
        bx = jnp.where(ok, a * b_sh + bx, bx)
        a = jnp.where(ok, a * a_sh, a)
        sh *= 2
    h = a * h_s[...] + bx
    y_ref[0] = h
    h_s[...] = h[0:1, :] if reverse else h[tb - 1:tb, :]

    @pl.when(i == nblk - 1)
    def _():
        hf_ref[0] = h_s[...]


def _lru_scan(xb, conv_w, conv_b, gate_w_bd, gate_b, lam, h0, direction):
    b, n, _ = xb.shape
    tb = TOKEN_BLOCK
    nblk = n // tb
    reverse = direction == 1
    idx = _blk_index(reverse, nblk)
    sub = tb // 8
    nsub = n // 8
    if reverse:
        prev_idx = lambda bb, i: (bb, jnp.maximum((nblk - 1 - i) * sub - 1, 0), 0)
        next_idx = lambda bb, i: (bb, jnp.minimum((nblk - i) * sub, nsub - 1), 0)
    else:
        prev_idx = lambda bb, i: (bb, jnp.maximum(i * sub - 1, 0), 0)
        next_idx = lambda bb, i: (bb, jnp.minimum((i + 1) * sub, nsub - 1), 0)
    return pl.pallas_call(
        functools.partial(_lru_kernel, reverse=reverse),
        out_shape=[jax.ShapeDtypeStruct((b, n, D_BRANCH), F32), jax.ShapeDtypeStruct(h0.shape, F32)],
        grid=(b, nblk),
        in_specs=[pl.BlockSpec((1, tb, D_BRANCH), idx),
                  pl.BlockSpec((1, 8, D_BRANCH), prev_idx),
                  pl.BlockSpec((1, 8, D_BRANCH), next_idx),
                  _const_spec((CONV_W, D_BRANCH)), _const_spec((1, D_BRANCH)),
                  _const_spec((2, D_BRANCH, D_BRANCH)), _const_spec((2, D_BRANCH)),
                  _const_spec((1, D_BRANCH)), _state_spec(h0.shape)],
        out_specs=[pl.BlockSpec((1, tb, D_BRANCH), idx), _state_spec(h0.shape)],
        scratch_shapes=[pltpu.VMEM((1, D_BRANCH), F32)],
        compiler_params=_params("parallel", "arbitrary"),
        name="rglru_scan",
    )(xb, xb, xb, conv_w, conv_b, gate_w_bd, gate_b, lam, h0)


def _rwkv_prep_kernel(s_ref, sp_ref, sn_ref, mu_ref, kk_ref, ka_ref, rk_ref, w0_ref, w2_ref,
                      a0_ref, a2_ref, ones_ref,
                      r_ref, v_ref, kkn_ref, bonus_ref, ld_ref, kt_ref, bb_ref, *, grid_shift):
    tb = TOKEN_BLOCK
    s = s_ref[0]
    width = s.shape[1]
    row = lax.broadcasted_iota(jnp.int32, (tb, width), 0)
    lane = lax.broadcasted_iota(jnp.int32, (tb, width), 1)
    if grid_shift:
        i = pl.program_id(1)
        nblk = pl.num_programs(1)
        col = row & (GRID_W - 1)
        qc = width // 4
        up_halo = sp_ref[0] * jnp.where(i > 0, 1.0, 0.0)
        dn_halo = sn_ref[0] * jnp.where(i < nblk - 1, 1.0, 0.0)
        left = jnp.where(col == 0, 0.0, pltpu.roll(s, 1, axis=0))
        right = jnp.where(col == GRID_W - 1, 0.0, pltpu.roll(s, tb - 1, axis=0))
        up = jnp.concatenate([up_halo, s[:tb - GRID_W]], axis=0)
        down = jnp.concatenate([s[GRID_W:], dn_halo], axis=0)
        sh = jnp.where(lane < qc, left,
                       jnp.where(lane < 2 * qc, right, jnp.where(lane < 3 * qc, up, down)))
    else:
        prev = jnp.where(row == 0, 0.0, pltpu.roll(s, 1, axis=0))
        nxt = jnp.where(row == tb - 1, 0.0, pltpu.roll(s, tb - 1, axis=0))
        sh = jnp.where(lane < width // 2, prev, nxt)
    s = s + mu_ref[...] * (sh - s)
    r = s[:, 0:D_BRANCH]
    k = s[:, D_BRANCH:2 * D_BRANCH]
    v = s[:, 2 * D_BRANCH:3 * D_BRANCH]
    lw = jnp.tanh(s[:, 3 * D_BRANCH:3 * D_BRANCH + LORA])
    la = s[:, 3 * D_BRANCH + LORA:3 * D_BRANCH + 2 * LORA]
    ones_bd = ones_ref[...]
    kkh = k * kk_ref[...]
    kk = kkh / jnp.maximum(jnp.sqrt(_head_sum(kkh * kkh, ones_bd)), 1e-12)
    r_ref[0] = r
    v_ref[0] = v
    kkn_ref[0] = kk
    bonus_ref[0] = _head_sum(r * k * rk_ref[...], ones_bd) * v
    for d in range(2):
        w_log = -_softplus(-(w0_ref[d:d + 1, :] + _dot(lw, w2_ref[d]))) - 0.5
        ld_ref[d, 0] = -jnp.exp(w_log)
        a = _sigmoid(a0_ref[d:d + 1, :] + _dot(la, a2_ref[d]))
        kt_ref[d, 0] = k * (1.0 + (a - 1.0) * ka_ref[...])
        bb_ref[d, 0] = kk * a


def _rwkv_prep(s, mu, k_k, k_a, r_k, w0, w2, a0, a2, ones_bd, grid_shift):
    b, n, width = s.shape
    tb = TOKEN_BLOCK
    nblk = n // tb
    assert grid_shift or nblk == 1
    halo = GRID_W
    sub = tb // halo
    nsub = n // halo
    tok = lambda bb, i: (bb, i, 0)
    tok2 = lambda bb, i: (0, bb, i, 0)
    out1 = jax.ShapeDtypeStruct((b, n, D_BRANCH), F32)
    out2 = jax.ShapeDtypeStruct((2, b, n, D_BRANCH), F32)
    return pl.pallas_call(
        functools.partial(_rwkv_prep_kernel, grid_shift=grid_shift),
        out_shape=[out1, out1, out1, out1, out2, out2, out2],
        grid=(b, nblk),
        in_specs=[pl.BlockSpec((1, tb, width), tok),
                  pl.BlockSpec((1, halo, width), lambda bb, i: (bb, jnp.maximum(i * sub - 1, 0), 0)),
                  pl.BlockSpec((1, halo, width), lambda bb, i: (bb, jnp.minimum((i + 1) * sub, nsub - 1), 0)),
                  _const_spec((1, width)), _const_spec((1, D_BRANCH)), _const_spec((1, D_BRANCH)),
                  _const_spec((1, D_BRANCH)), _const_spec((2, D_BRANCH)),
                  _const_spec((2, LORA, D_BRANCH)), _const_spec((2, D_BRANCH)),
                  _const_spec((2, LORA, D_BRANCH)), _const_spec((D_BRANCH, D_BRANCH))],
        out_specs=[pl.BlockSpec((1, tb, D_BRANCH), tok)] * 4
                  + [pl.BlockSpec((2, 1, tb, D_BRANCH), tok2)] * 3,
        compiler_params=_params("parallel", "parallel"),
        name="rwkv_prep",
    )(s, s, s, mu, k_k, k_a, r_k, w0, w2, a0, a2, ones_bd)


def _rwkv_kernel(r_ref, v_ref, kk_ref, ld_ref, kt_ref, bb_ref, s0_ref, y_ref, sf_ref, s_s, *, reverse):
    @pl.when(pl.program_id(1) == 0)
    def _():
        s_s[...] = s0_ref[0]

    L = CHUNK
    nchunk = TOKEN_BLOCK // L
    incl, strict, row, col = _tri_masks(L, reverse)
    incl_f = jnp.where(incl, 1.0, 0.0).astype(F32)
    eye_l = jnp.where(row == col, 1.0, 0.0).astype(F32)
    same16 = (row >> 4) == (col >> 4)
    same32 = (row >> 5) == (col >> 5)
    off32 = jnp.logical_and(same32, jnp.logical_not(same16))
    off64 = jnp.logical_not(same32)
    last = 0 if reverse else L - 1

    def chunk(ci, carry):
        cidx = (nchunk - 1 - ci) if reverse else ci
        r0 = pl.multiple_of(cidx * L, L)
        rows = pl.ds(r0, L)
        ld = ld_ref[0, 0, rows, :]
        cs = _dot_hi(incl_f, ld)
        g_in = jnp.exp(cs)
        g_inv = jnp.exp(-cs)
        kap = kk_ref[0, rows, :] * jnp.exp(cs - ld)
        bet = bb_ref[0, 0, rows, :] * g_inv
        kti = kt_ref[0, 0, rows, :] * g_inv
        rti = r_ref[0, rows, :] * g_in
        g_last = g_in[last:last + 1, :]
        for h in range(N_HEADS):
            sl = slice(h * HEAD_DIM, (h + 1) * HEAD_DIM)
            kap_h, bet_h, kti_h, rti_h = kap[:, sl], bet[:, sl], kti[:, sl], rti[:, sl]
            vh = v_ref[0, rows, sl]
            s_prev = s_s[h]
            x2 = jnp.concatenate([kap_h, rti_h], axis=0)
            p_b = _dot_nt_hi(x2, bet_h)
            p_k = _dot_nt_hi(x2, kti_h)
            p_s = _dot_nt_hi(x2, s_prev)
            a_ub = jnp.where(strict, p_b[:L], 0.0)
            a_uk = jnp.where(strict, p_k[:L], 0.0)
            a_rb = jnp.where(incl, p_b[L:], 0.0)
            a_rk = jnp.where(incl, p_k[L:], 0.0)
            pw = jnp.where(same16, -a_ub, 0.0)
            tinv = eye_l + pw
            for _ in range(3):
                pw = _dot_hi(pw, pw)
                tinv = tinv + _dot_hi(tinv, pw)
            for off in (off32, off64):
                tinv = tinv - _dot_hi(tinv, _dot_hi(jnp.where(off, a_ub, 0.0), tinv))
            uh = -_dot_hi(tinv, p_s[:L] + _dot_hi(a_uk, vh))
            y_ref[0, rows, sl] = p_s[L:] + _dot_hi(a_rb, uh) + _dot_hi(a_rk, vh)
            upd = (lax.dot_general(uh, bet_h, (((0,), (0,)), ((), ())), preferred_element_type=F32,
                                   precision=HIGHEST)
                   + lax.dot_general(vh, kti_h, (((0,), (0,)), ((), ())), preferred_element_type=F32,
                                     precision=HIGHEST))
            s_s[h] = (s_prev + upd) * g_last[:, sl]
        return carry

    lax.fori_loop(0, nchunk, chunk, 0)

    @pl.when(pl.program_id(1) == pl.num_programs(1) - 1)
    def _():
        sf_ref[0] = s_s[...]


def _rwkv_scan(r, v, kk, ld, kt, bb, s0, direction):
    b, n, _ = r.shape
    nblk = n // TOKEN_BLOCK
    reverse = direction == 1
    idx = _blk_index(reverse, nblk)
    if reverse:
        idx2 = lambda bb_, i: (direction, bb_, nblk - 1 - i, 0)
    else:
        idx2 = lambda bb_, i: (direction, bb_, i, 0)
    tok = pl.BlockSpec((1, TOKEN_BLOCK, D_BRANCH), idx)
    tok2 = pl.BlockSpec((1, 1, TOKEN_BLOCK, D_BRANCH), idx2)
    return pl.pallas_call(
        functools.partial(_rwkv_kernel, reverse=reverse),
        out_shape=[jax.ShapeDtypeStruct((b, n, D_BRANCH), F32), jax.ShapeDtypeStruct(s0.shape, F32)],
        grid=(b, nblk),
        in_specs=[tok, tok, tok, tok2, tok2, tok2, _state_spec(s0.shape)],
        out_specs=[tok, _state_spec(s0.shape)],
        scratch_shapes=[pltpu.VMEM(s0.shape[1:], F32)],
        compiler_params=_params("parallel", "arbitrary"),
        name="rwkv_scan",
    )(r, v, kk, ld, kt, bb, s0)


def _combine_kernel(x_ref, mod_ref, maf_ref, mab_ref, og_ref, lf_ref, lb_ref, lg_ref,
                    rf_ref, rb_ref, rg_ref, wf_ref, wb_ref, bonus_ref, wg_ref, ones_ref, wo_ref,
                    *rest, final):
    if final:
        fg_ref, o_ref = rest
    else:
        (o_ref,) = rest
    ones_bd = ones_ref[...]
    og = og_ref[0]
    h_a = (maf_ref[0] + mab_ref[0]) * _sigmoid(og[:, :D_BRANCH])
    y_a = _head_norm(h_a, ones_bd) * _silu(og[:, D_BRANCH:])
    y_b = (lf_ref[0] + lb_ref[0]) * _silu(lg_ref[0])
    y_c = _head_norm(rf_ref[0] + rb_ref[0], ones_bd) * _silu(rg_ref[0])
    y_d = (_head_norm(wf_ref[0] + wb_ref[0], ones_bd) + bonus_ref[0]) * _silu(wg_ref[0])
    y = (_dot(y_a, wo_ref[0:D_BRANCH, :]) + _dot(y_b, wo_ref[D_BRANCH:2 * D_BRANCH, :])
         + _dot(y_c, wo_ref[2 * D_BRANCH:3 * D_BRANCH, :]) + _dot(y_d, wo_ref[3 * D_BRANCH:, :]))
    x = x_ref[0] + mod_ref[0, 2:3, :] * y
    if final:
        x = x * lax.rsqrt(jnp.mean(x * x, axis=-1, keepdims=True) + EPS) * fg_ref[...]
    o_ref[0] = x


def _combine(x, mod, branch_arrays, ones_bd, w_out, final_g):
    b, n, _ = x.shape
    tm = TOKEN_BLOCK
    tok = lambda i, j: (i, j, 0)
    final = final_g is not None
    in_specs = [pl.BlockSpec((1, tm, D_MODEL), tok), pl.BlockSpec((1, 3, D_MODEL), lambda i, j: (i, 0, 0))]
    in_specs += [pl.BlockSpec((1, tm, a.shape[-1]), tok) for a in branch_arrays]
    in_specs += [_const_spec((D_BRANCH, D_BRANCH)), _const_spec(w_out.shape)]
    args = [x, mod, *branch_arrays, ones_bd, w_out]
    if final:
        in_specs.append(_const_spec((1, D_MODEL)))
        args.append(final_g)
    return pl.pallas_call(
        functools.partial(_combine_kernel, final=final),
        out_shape=jax.ShapeDtypeStruct(x.shape, F32),
        grid=(b, n // tm),
        in_specs=in_specs,
        out_specs=pl.BlockSpec((1, tm, D_MODEL), tok),
        compiler_params=_params("parallel", "parallel"),
        name="combine_outproj",
    )(*args)


def _prepare_weights(w_in, mlstm_gate_b, lru_gate_w):
    p_mlstm = 5 * D_BRANCH + 4 * N_HEADS
    a0, a1 = 0, p_mlstm
    b1 = a1 + 2 * D_BRANCH
    c1 = b1 + 4 * D_BRANCH
    wa = w_in[:, :, a0:a1]
    pad = jnp.zeros((DEPTH, D_MODEL, LANES - 4 * N_HEADS), w_in.dtype)
    w_a = jnp.concatenate([wa[:, :, :3 * D_BRANCH], wa[:, :, 5 * D_BRANCH:], pad,
                           wa[:, :, 3 * D_BRANCH:5 * D_BRANCH]], axis=-1).astype(BF16)
    w_b = w_in[:, :, a1:b1].astype(BF16)
    w_c = w_in[:, :, b1:c1].astype(BF16)
    w_d = w_in[:, :, c1:].astype(BF16)
    gate_b = jnp.pad(mlstm_gate_b, ((0, 0), (0, LANES - 4 * N_HEADS)))[:, None, :]
    eye_h = jnp.eye(N_HEADS, dtype=lru_gate_w.dtype)
    gw_bd = jnp.einsum('ldghij,hk->ldghikj', lru_gate_w, eye_h).reshape(
        DEPTH, 2, 2, D_BRANCH, D_BRANCH).astype(BF16)
    return w_a, w_b, w_c, w_d, gate_b, gw_bd


def _rope_tables(n):
    rows = n // GRID_W
    row_idx = jnp.broadcast_to(jnp.arange(rows, dtype=F32)[:, None], (rows, GRID_W)).reshape(-1)
    col_idx = jnp.broadcast_to(jnp.arange(GRID_W, dtype=F32)[None, :], (rows, GRID_W)).reshape(-1)
    n_freq = HEAD_DIM // 4
    freqs = ROPE_BASE ** (-jnp.arange(n_freq, dtype=F32) / n_freq)
    ang = jnp.concatenate([row_idx[:, None] * freqs, col_idx[:, None] * freqs], -1)
    cos, sin = jnp.cos(ang), jnp.sin(ang)
    cos_t = jnp.concatenate([cos, cos] * (LANES // HEAD_DIM), axis=-1)
    sin_t = jnp.concatenate([-sin, sin] * (LANES // HEAD_DIM), axis=-1)
    return cos_t, sin_t


def _layer(x, mod, lw, states, latent, rope_tabs, ones_bd, final_g):
    c0, n0, m0, h0, r0, s0 = states
    norm_g = lw['norm_g']
    u_a, og_a = _inproj(x, norm_g, mod, lw['w_a'], (3 * D_BRANCH + LANES, 2 * D_BRANCH))
    x_b, g_b = _inproj(x, norm_g, mod, lw['w_b'], (D_BRANCH, D_BRANCH))
    u_c, g_c = _inproj(x, norm_g, mod, lw['w_c'], (3 * D_BRANCH, D_BRANCH))
    s_d, g_d = _inproj(x, norm_g, mod, lw['w_d'], (RWKV_SHIFT, D_BRANCH))

    prep = _rwkv_prep(s_d, lw['rwkv_mu'], lw['rwkv_kk'], lw['rwkv_ka'], lw['rwkv_rk'], lw['rwkv_w0'],
                      lw['rwkv_w2'], lw['rwkv_a0'], lw['rwkv_a2'], ones_bd, latent)
    r_d, v_d, kk_d, bonus_d, ld_d, kt_d, bb_d = prep

    ys, finals = [], []
    for d in range(2):
        ya, cf, nf, mf = _mlstm_scan(u_a, lw['gate_b'], c0[:, d], n0[:, d][:, :, None, :],
                                     m0[:, d][:, None, :], d)
        yb, hf = _lru_scan(x_b, lw['lru_conv_w'], lw['lru_conv_b'], lw['gw_bd'][d], lw['lru_gate_b'][d],
                           lw['lru_lambda'][d][None, :], h0[:, d][:, None, :], d)
        yc, rf = _ret_scan(u_c, lw['theta'], r0[:, d], d, rope_tabs)
        yd, sf = _rwkv_scan(r_d, v_d, kk_d, ld_d, kt_d, bb_d, s0[:, d], d)
        ys.append((ya, yb, yc, yd))
        finals.append((cf, nf[:, :, 0, :], mf[:, 0, :], hf[:, 0, :], rf, sf))
    (yaf, ybf, ycf, ydf), (yab, ybb, ycb, ydb) = ys
    branch_arrays = [yaf, yab, og_a, ybf, ybb, g_b, ycf, ycb, g_c, ydf, ydb, bonus_d, g_d]
    x_new = _combine(x, mod, branch_arrays, ones_bd, lw['w_out'], final_g)
    new_states = tuple(jnp.stack([finals[0][j], finals[1][j]], axis=1) for j in range(6))
    return x_new, new_states


def kernel(x_prompt, x_sample, c, state_mlstm_c, state_mlstm_n, state_mlstm_m, state_lru_h, state_ret_r,
           state_rwkv_s, c_ctx, norm_g, w_mod, b_mod, w_in, w_out, mlstm_gate_b, lru_conv_w, lru_conv_b,
           lru_gate_w, lru_gate_b, lru_lambda, ret_theta, rwkv_mu, rwkv_w0, rwkv_w2, rwkv_a0, rwkv_a2,
           rwkv_kk, rwkv_ka, rwkv_rk, final_g):
    bp = x_prompt.shape[0]
    bs = x_sample.shape[0]
    assert x_prompt.shape[1] == TOKEN_BLOCK and x_sample.shape[1] % TOKEN_BLOCK == 0
    assert 1 + bs <= 8

    w_a, w_b, w_c, w_d, gate_b, gw_bd = _prepare_weights(w_in, mlstm_gate_b, lru_gate_w)
    w_out_bf = w_out.astype(BF16)
    head_of = jnp.arange(D_BRANCH) // HEAD_DIM
    ones_bd = (head_of[:, None] == head_of[None, :]).astype(BF16)
    theta = jnp.pad(ret_theta.reshape(DEPTH, 1, 2 * N_HEADS), ((0, 0), (0, 0), (0, LANES - 2 * N_HEADS)))
    rope_tabs = _rope_tables(x_sample.shape[1])

    cvec = jnp.concatenate([c_ctx[None, :], c, jnp.zeros((8 - 1 - bs, D_MODEL), F32)], axis=0)
    mods = _modulation(cvec, w_mod, b_mod).reshape(DEPTH, 8, 3, D_MODEL)

    def layer_weights(l):
        return dict(norm_g=norm_g[l][None, :], w_a=w_a[l], w_b=w_b[l], w_c=w_c[l], w_d=w_d[l],
                    gate_b=gate_b[l], gw_bd=gw_bd[l], lru_conv_w=lru_conv_w[l],
                    lru_conv_b=lru_conv_b[l][None, :], lru_gate_b=lru_gate_b[l], lru_lambda=lru_lambda[l],
                    theta=theta[l], rwkv_mu=rwkv_mu[l][None, :], rwkv_kk=rwkv_kk[l][None, :],
                    rwkv_ka=rwkv_ka[l][None, :], rwkv_rk=rwkv_rk[l][None, :], rwkv_w0=rwkv_w0[l],
                    rwkv_w2=rwkv_w2[l].astype(BF16), rwkv_a0=rwkv_a0[l], rwkv_a2=rwkv_a2[l].astype(BF16),
                    w_out=w_out_bf[l])

    zero_states = (jnp.zeros((bp, 2, N_HEADS, HEAD_DIM, HEAD_DIM), F32),
                   jnp.zeros((bp, 2, N_HEADS, HEAD_DIM), F32),
                   jnp.zeros((bp, 2, N_HEADS), F32),
                   jnp.zeros((bp, 2, D_BRANCH), F32),
                   jnp.zeros((bp, 2, N_HEADS, HEAD_DIM, HEAD_DIM), F32),
                   jnp.zeros((bp, 2, N_HEADS, HEAD_DIM, HEAD_DIM), F32))
    xp = x_prompt
    per_layer = []
    for l in range(DEPTH):
        mod = jnp.broadcast_to(mods[l, 0][None], (bp, 3, D_MODEL))
        xp, st = _layer(xp, mod, layer_weights(l), zero_states, False, None, ones_bd,
                        final_g[None, :] if l == DEPTH - 1 else None)
        per_layer.append(st)
    new_states = tuple(jnp.stack([st[j] for st in per_layer], axis=1) for j in range(6))

    xs = x_sample
    for l in range(DEPTH):
        mod = mods[l, 1:1 + bs]
        st = (state_mlstm_c[:, l], state_mlstm_n[:, l], state_mlstm_m[:, l], state_lru_h[:, l],
              state_ret_r[:, l], state_rwkv_s[:, l])
        xs, _ = _layer(xs, mod, layer_weights(l), st, True, rope_tabs, ones_bd,
                       final_g[None, :] if l == DEPTH - 1 else None)
    return (xp, xs) + new_states
```

```python
import functools

import jax
import jax.numpy as jnp
from jax import lax
from jax.experimental import pallas as pl
from jax.experimental.pallas import tpu as pltpu

F32 = jnp.float32
BF16 = jnp.bfloat16
HIGHEST = lax.Precision.HIGHEST

D_MODEL = 1024
DEPTH = 4
GRID_W = 64
D_BRANCH = 512
HEAD_DIM = 64
N_HEADS = 8
CHUNK = 64
CONV_W = 4
LRU_C = 8.0
LORA = 64
ROPE_BASE = 100.0
EPS = 1e-6
RWKV_SHIFT = 3 * D_BRANCH + 2 * LORA

LANES = 128
TOKEN_BLOCK = 256
VMEM_LIMIT = 48 * 1024 * 1024
NEG = -1e30


def _sigmoid(x):
    return 1.0 / (1.0 + jnp.exp(-x))


def _silu(x):
    return x * _sigmoid(x)


def _softplus(x):
    return jnp.maximum(x, 0.0) + jnp.log(1.0 + jnp.exp(-jnp.abs(x)))


def _log_sigmoid(x):
    return -_softplus(-x)


def _dot(a, b):
    return jnp.dot(a.astype(BF16), b.astype(BF16), preferred_element_type=F32)


def _dot_nt(a, b):
    return lax.dot_general(a.astype(BF16), b.astype(BF16), (((1,), (1,)), ((), ())),
                           preferred_element_type=F32)


def _dot_tn(a, b):
    return lax.dot_general(a.astype(BF16), b.astype(BF16), (((0,), (0,)), ((), ())),
                           preferred_element_type=F32)


def _dot_hi(a, b):
    return jnp.dot(a, b, preferred_element_type=F32, precision=HIGHEST)


def _dot_nt_hi(a, b):
    return lax.dot_general(a, b, (((1,), (1,)), ((), ())), preferred_element_type=F32,
                           precision=HIGHEST)


_NN = (((1,), (0,)), ((), ()))
_NT = (((1,), (1,)), ((), ()))
_TN = (((0,), (0,)), ((), ()))
_BNN = (((2,), (1,)), ((0,), (0,)))
_BNT = (((2,), (2,)), ((0,), (0,)))
_RWKV_PREC = dict(a=1, inv=3, merge=1, solve=3, rest=1)


def _split_heads(x):
    return jnp.stack([x[:, h * HEAD_DIM:(h + 1) * HEAD_DIM] for h in range(N_HEADS)], axis=0)


def _mm(a, b, dims, passes):
    if passes == 6:
        return lax.dot_general(a, b, dims, preferred_element_type=F32, precision=HIGHEST)
    ah = a.astype(BF16)
    bh = b.astype(BF16)
    out = lax.dot_general(ah, bh, dims, preferred_element_type=F32)
    if passes == 3:
        al = (a - ah.astype(F32)).astype(BF16)
        bl = (b - bh.astype(F32)).astype(BF16)
        out = (out + lax.dot_general(ah, bl, dims, preferred_element_type=F32)
               + lax.dot_general(al, bh, dims, preferred_element_type=F32))
    return out


def _head_sum(x, ones_bd):
    hi = x.astype(BF16)
    lo = (x - hi.astype(F32)).astype(BF16)
    return (jnp.dot(hi, ones_bd, preferred_element_type=F32)
            + jnp.dot(lo, ones_bd, preferred_element_type=F32))


def _head_norm(x, ones_bd):
    return x * lax.rsqrt(_head_sum(x * x, ones_bd) * (1.0 / HEAD_DIM) + EPS)


def _tri_masks(n, reverse):
    row = lax.broadcasted_iota(jnp.int32, (n, n), 0)
    col = lax.broadcasted_iota(jnp.int32, (n, n), 1)
    incl = (col >= row) if reverse else (col <= row)
    strict = (col > row) if reverse else (col < row)
    return incl, strict, row, col


def _eye(n):
    row = lax.broadcasted_iota(jnp.int32, (n, n), 0)
    col = lax.broadcasted_iota(jnp.int32, (n, n), 1)
    return jnp.where(row == col, 1.0, 0.0).astype(F32)


def _params(*sem):
    return pltpu.CompilerParams(dimension_semantics=sem, vmem_limit_bytes=VMEM_LIMIT)


def _blk_index(reverse, nblk):
    if reverse:
        return lambda b, i: (b, nblk - 1 - i, 0)
    return lambda b, i: (b, i, 0)


def _state_spec(shape):
    nd = len(shape)
    return pl.BlockSpec((1,) + tuple(shape[1:]), lambda b, i: (b,) + (0,) * (nd - 1))


def _const_spec(shape):
    nd = len(shape)
    return pl.BlockSpec(tuple(shape), lambda *_: (0,) * nd)


def _mod_kernel(c_ref, w_ref, b_ref, o_ref):
    sc = _silu(c_ref[...])
    o_ref[0] = _dot(sc, w_ref[0]) + b_ref[0]


def _modulation(cvec, w_mod, b_mod):
    nt = 3
    return pl.pallas_call(
        _mod_kernel,
        out_shape=jax.ShapeDtypeStruct((DEPTH, 8, 3 * D_MODEL), F32),
        grid=(DEPTH, nt),
        in_specs=[pl.BlockSpec((8, D_MODEL), lambda l, j: (0, 0)),
                  pl.BlockSpec((1, D_MODEL, D_MODEL), lambda l, j: (l, 0, j)),
                  pl.BlockSpec((1, 1, D_MODEL), lambda l, j: (l, 0, j))],
        out_specs=pl.BlockSpec((1, 8, D_MODEL), lambda l, j: (l, 0, j)),
        compiler_params=_params("arbitrary", "arbitrary"),
        name="modulation",
    )(cvec, w_mod, b_mod.reshape(DEPTH, 1, 3 * D_MODEL))


def _inproj_kernel(x_ref, g_ref, mod_ref, w_ref, *o_refs, widths):
    x = x_ref[0]
    y = x * lax.rsqrt(jnp.mean(x * x, axis=-1, keepdims=True) + EPS) * g_ref[...]
    h = y * (1.0 + mod_ref[0, 1:2, :]) + mod_ref[0, 0:1, :]
    u = _dot(h, w_ref[...])
    off = 0
    for o_ref, w in zip(o_refs, widths):
        o_ref[0] = u[:, off:off + w]
        off += w


def _inproj(x, g, mod, w, widths):
    b, n, _ = x.shape
    p = w.shape[1]
    tm = TOKEN_BLOCK
    return pl.pallas_call(
        functools.partial(_inproj_kernel, widths=widths),
        out_shape=[jax.ShapeDtypeStruct((b, n, wd), F32) for wd in widths],
        grid=(b, n // tm),
        in_specs=[pl.BlockSpec((1, tm, D_MODEL), lambda i, j: (i, j, 0)),
                  _const_spec((1, D_MODEL)),
                  pl.BlockSpec((1, 3, D_MODEL), lambda i, j: (i, 0, 0)),
                  _const_spec((D_MODEL, p))],
        out_specs=[pl.BlockSpec((1, tm, wd), lambda i, j: (i, j, 0)) for wd in widths],
        compiler_params=_params("parallel", "parallel"),
        name="inproj",
    )(x, g, mod, w)


def _mlstm_kernel(u_ref, gb_ref, c0_ref, n0_ref, m0_ref, y_ref, cf_ref, nf_ref, mf_ref,
                  c_s, n_s, m_s, *, reverse, direction):
    @pl.when(pl.program_id(1) == 0)
    def _():
        c_s[...] = c0_ref[0]
        n_s[...] = n0_ref[0]
        m_s[...] = m0_ref[0]

    L = CHUNK
    nchunk = TOKEN_BLOCK // L
    incl, _, _, _ = _tri_masks(L, reverse)
    incl_f = jnp.where(incl, 1.0, 0.0).astype(F32)
    eye = _eye(LANES)
    ig_off = 2 * N_HEADS * direction
    fg_off = ig_off + N_HEADS
    last = 0 if reverse else L - 1

    def chunk(ci, carry):
        cidx = (nchunk - 1 - ci) if reverse else ci
        r0 = pl.multiple_of(cidx * L, L)
        rows = pl.ds(r0, L)
        gates = u_ref[0, rows, 3 * D_BRANCH:3 * D_BRANCH + LANES] + gb_ref[...]
        lf = _log_sigmoid(gates)
        bmat = _dot_hi(incl_f, lf)
        g_t = _dot_nt_hi(eye, gates)
        b_t = _dot_nt_hi(eye, bmat)
        q = _split_heads(u_ref[0, rows, 0:D_BRANCH])
        k = _split_heads(u_ref[0, rows, D_BRANCH:2 * D_BRANCH])
        v = _split_heads(u_ref[0, rows, 2 * D_BRANCH:3 * D_BRANCH])
        heads = range(N_HEADS)
        b_col = jnp.stack([bmat[:, fg_off + h:fg_off + h + 1] for h in heads], axis=0)
        ig_col = jnp.stack([gates[:, ig_off + h:ig_off + h + 1] for h in heads], axis=0)
        b_row = jnp.stack([b_t[fg_off + h:fg_off + h + 1, :] for h in heads], axis=0)
        ig_row = jnp.stack([g_t[ig_off + h:ig_off + h + 1, :] for h in heads], axis=0)
        c_prev = c_s[...]
        n_prev = n_s[...]
        m_prev = m_s[...]
        dmat = jnp.where(incl, b_col - b_row + ig_row, NEG)
        m_inter = b_col + m_prev
        m_t = jnp.maximum(m_inter, jnp.max(dmat, axis=2, keepdims=True))
        s = _mm(q, k, _BNT, 1) * (HEAD_DIM ** -0.5) * jnp.exp(dmat - m_t)
        w_inter = jnp.exp(m_inter - m_t)
        num = _mm(s, v, _BNN, 1) + w_inter * _mm(q, c_prev, _BNN, 1)
        den = (jnp.sum(s, axis=2, keepdims=True)
               + w_inter * jnp.sum(q * n_prev, axis=2, keepdims=True))
        hh = num / jnp.maximum(jnp.abs(den), jnp.exp(-m_t))
        b_last = b_col[:, last:last + 1, :]
        w_log = b_last - b_col + ig_col
        m_new = jnp.maximum(b_last + m_prev, jnp.max(w_log, axis=1, keepdims=True))
        dec = jnp.exp(b_last + m_prev - m_new)
        wk = k * (HEAD_DIM ** -0.5) * jnp.exp(w_log - m_new)
        n_s[...] = dec * n_prev + jnp.sum(wk, axis=1, keepdims=True)
        m_s[...] = m_new
        for h in heads:
            y_ref[0, rows, h * HEAD_DIM:(h + 1) * HEAD_DIM] = hh[h]
            c_s[h] = dec[h] * c_prev[h] + _dot_tn(wk[h], v[h])
        return carry

    lax.fori_loop(0, nchunk, chunk, 0)

    @pl.when(pl.program_id(1) == pl.num_programs(1) - 1)
    def _():
        cf_ref[0] = c_s[...]
        nf_ref[0] = n_s[...]
        mf_ref[0] = m_s[...]


def _mlstm_scan(u, gate_b, c0, n0, m0, direction):
    b, n, p = u.shape
    nblk = n // TOKEN_BLOCK
    reverse = direction == 1
    idx = _blk_index(reverse, nblk)
    return pl.pallas_call(
        functools.partial(_mlstm_kernel, reverse=reverse, direction=direction),
        out_shape=[jax.ShapeDtypeStruct((b, n, D_BRANCH), F32),
                   jax.ShapeDtypeStruct(c0.shape, F32),
                   jax.ShapeDtypeStruct(n0.shape, F32),
                   jax.ShapeDtypeStruct(m0.shape, F32)],
        grid=(b, nblk),
        in_specs=[pl.BlockSpec((1, TOKEN_BLOCK, p), idx),
                  _const_spec((1, LANES)),
                  _state_spec(c0.shape), _state_spec(n0.shape), _state_spec(m0.shape)],
        out_specs=[pl.BlockSpec((1, TOKEN_BLOCK, D_BRANCH), idx),
                   _state_spec(c0.shape), _state_spec(n0.shape), _state_spec(m0.shape)],
        scratch_shapes=[pltpu.VMEM(c0.shape[1:], F32), pltpu.VMEM(n0.shape[1:], F32),
                        pltpu.VMEM(m0.shape[1:], F32)],
        compiler_params=_params("parallel", "arbitrary"),
        name="mlstm_scan",
    )(u, gate_b, c0, n0, m0)


def _ret_kernel(*refs, reverse, direction, rope):
    if rope:
        u_ref, th_ref, cos_ref, sin_ref, r0_ref, y_ref, rf_ref, r_s = refs
    else:
        u_ref, th_ref, r0_ref, y_ref, rf_ref, r_s = refs

    @pl.when(pl.program_id(1) == 0)
    def _():
        r_s[...] = r0_ref[0]

    L = CHUNK
    nchunk = TOKEN_BLOCK // L
    incl, _, row, col = _tri_masks(L, reverse)
    diff = jnp.abs(row - col).astype(F32)
    pos_col = lax.broadcasted_iota(jnp.int32, (L, 1), 0).astype(F32)
    p_col = (L - 1.0 - pos_col) if reverse else pos_col
    log_g = _log_sigmoid(th_ref[...])
    lg = jnp.stack([log_g[:, N_HEADS * direction + h:N_HEADS * direction + h + 1]
                    for h in range(N_HEADS)], axis=0)
    dmat = jnp.where(incl, jnp.exp(lg * diff), 0.0)
    xi = jnp.exp(lg * (p_col + 1.0))
    wk = jnp.exp(lg * (L - 1.0 - p_col))
    dec = jnp.exp(lg * float(L))
    lane = lax.broadcasted_iota(jnp.int32, (L, D_BRANCH), 1)
    first_half = (lane & (HEAD_DIM - 1)) < (HEAD_DIM // 2)

    def rot(x, c, s):
        swapped = jnp.where(first_half, pltpu.roll(x, D_BRANCH - HEAD_DIM // 2, axis=1),
                            pltpu.roll(x, HEAD_DIM // 2, axis=1))
        return x * c + swapped * s

    def chunk(ci, carry):
        cidx = (nchunk - 1 - ci) if reverse else ci
        r0 = pl.multiple_of(cidx * L, L)
        rows = pl.ds(r0, L)
        q = u_ref[0, rows, 0:D_BRANCH]
        k = u_ref[0, rows, D_BRANCH:2 * D_BRANCH]
        if rope:
            c = jnp.concatenate([cos_ref[rows, :]] * (D_BRANCH // LANES), axis=1)
            s = jnp.concatenate([sin_ref[rows, :]] * (D_BRANCH // LANES), axis=1)
            q = rot(q, c, s)
            k = rot(k, c, s)
        qh = _split_heads(q)
        kh = _split_heads(k) * (HEAD_DIM ** -0.5)
        vh = _split_heads(u_ref[0, rows, 2 * D_BRANCH:3 * D_BRANCH])
        r_prev = r_s[...]
        s_mat = _mm(qh, kh, _BNT, 1) * dmat
        y = _mm(s_mat, vh, _BNN, 1) + xi * _mm(qh, r_prev, _BNN, 1)
        kw = kh * wk
        for h in range(N_HEADS):
            y_ref[0, rows, h * HEAD_DIM:(h + 1) * HEAD_DIM] = y[h]
            r_s[h] = dec[h] * r_prev[h] + _dot_tn(kw[h], vh[h])
        return carry

    lax.fori_loop(0, nchunk, chunk, 0)

    @pl.when(pl.program_id(1) == pl.num_programs(1) - 1)
    def _():
        rf_ref[0] = r_s[...]


def _ret_scan(u, theta, r0, direction, rope_tabs):
    b, n, p = u.shape
    nblk = n // TOKEN_BLOCK
    reverse = direction == 1
    idx = _blk_index(reverse, nblk)
    rope = rope_tabs is not None
    in_specs = [pl.BlockSpec((1, TOKEN_BLOCK, p), idx), _const_spec((1, LANES))]
    args = [u, theta]
    if rope:
        tab_idx = (lambda bb, i: (nblk - 1 - i, 0)) if reverse else (lambda bb, i: (i, 0))
        in_specs += [pl.BlockSpec((TOKEN_BLOCK, LANES), tab_idx)] * 2
        args += list(rope_tabs)
    in_specs.append(_state_spec(r0.shape))
    args.append(r0)
    return pl.pallas_call(
        functools.partial(_ret_kernel, reverse=reverse, direction=direction, rope=rope),
        out_shape=[jax.ShapeDtypeStruct((b, n, D_BRANCH), F32), jax.ShapeDtypeStruct(r0.shape, F32)],
        grid=(b, nblk),
        in_specs=in_specs,
        out_specs=[pl.BlockSpec((1, TOKEN_BLOCK, D_BRANCH), idx), _state_spec(r0.shape)],
        scratch_shapes=[pltpu.VMEM(r0.shape[1:], F32)],
        compiler_params=_params("parallel", "arbitrary"),
        name="retention_scan",
    )(*args)


def _lru_kernel(x_ref, xp_ref, xn_ref, cw_ref, cb_ref, gw_ref, gb_ref, lam_ref, h0_ref,
                y_ref, hf_ref, h_s, *, reverse):
    i = pl.program_id(1)
    nblk = pl.num_programs(1)
    blk = (nblk - 1 - i) if reverse else i

    @pl.when(i == 0)
    def _():
        h_s[...] = h0_ref[0]

    tb = TOKEN_BLOCK
    x = x_ref[0]
    row = lax.broadcasted_iota(jnp.int32, (tb, D_BRANCH), 0)
    prev_ok = jnp.where(blk > 0, 1.0, 0.0)
    next_ok = jnp.where(blk < nblk - 1, 1.0, 0.0)
    p_last = xp_ref[0, 7:8, :] * prev_ok
    n_0 = xn_ref[0, 0:1, :] * next_ok
    n_1 = xn_ref[0, 1:2, :] * next_ok
    xm1 = jnp.where(row == 0, p_last, pltpu.roll(x, 1, axis=0))
    xp1 = jnp.where(row == tb - 1, n_0, pltpu.roll(x, tb - 1, axis=0))
    xp2 = jnp.where(row == tb - 1, n_1, jnp.where(row == tb - 2, n_0, pltpu.roll(x, tb - 2, axis=0)))
    xc = (cw_ref[0:1, :] * xm1 + cw_ref[1:2, :] * x + cw_ref[2:3, :] * xp1 + cw_ref[3:4, :] * xp2
          + cb_ref[...])
    gr = _dot(xc, gw_ref[0]) + gb_ref[0:1, :]
    gi = _dot(xc, gw_ref[1]) + gb_ref[1:2, :]
    log_a = -LRU_C * _sigmoid(gr) * _softplus(-lam_ref[...])
    a = jnp.exp(log_a)
    beta = jnp.sqrt(1.0 - jnp.exp(2.0 * log_a))
    bx = beta * _sigmoid(gi) * xc

    sh = 1
    while sh < tb:
        if reverse:
            ok = row < tb - sh
            a_sh = pltpu.roll(a, tb - sh, axis=0)
            b_sh = pltpu.roll(bx, tb - sh, axis=0)
        else:
            ok = row >= sh
            a_sh = pltpu.roll(a, sh, axis=0)
            b_sh = pltpu.roll(bx, sh, axis=0)
        bx = jnp.where(ok, a * b_sh + bx, bx)
        a = jnp.where(ok, a * a_sh, a)
        sh *= 2
    h = a * h_s[...] + bx
    y_ref[0] = h
    h_s[...] = h[0:1, :] if reverse else h[tb - 1:tb, :]

    @pl.when(i == nblk - 1)
    def _():
        hf_ref[0] = h_s[...]


def _lru_scan(xb, conv_w, conv_b, gate_w_bd, gate_b, lam, h0, direction):
    b, n, _ = xb.shape
    tb = TOKEN_BLOCK
    nblk = n // tb
    reverse = direction == 1
    idx = _blk_index(reverse, nblk)
    sub = tb // 8
    nsub = n // 8
    if reverse:
        prev_idx = lambda bb, i: (bb, jnp.maximum((nblk - 1 - i) * sub - 1, 0), 0)
        next_idx = lambda bb, i: (bb, jnp.minimum((nblk - i) * sub, nsub - 1), 0)
    else:
        prev_idx = lambda bb, i: (bb, jnp.maximum(i * sub - 1, 0), 0)
        next_idx = lambda bb, i: (bb, jnp.minimum((i + 1) * sub, nsub - 1), 0)
    return pl.pallas_call(
        functools.partial(_lru_kernel, reverse=reverse),
        out_shape=[jax.ShapeDtypeStruct((b, n, D_BRANCH), F32), jax.ShapeDtypeStruct(h0.shape, F32)],
        grid=(b, nblk),
        in_specs=[pl.BlockSpec((1, tb, D_BRANCH), idx),
                  pl.BlockSpec((1, 8, D_BRANCH), prev_idx),
                  pl.BlockSpec((1, 8, D_BRANCH), next_idx),
                  _const_spec((CONV_W, D_BRANCH)), _const_spec((1, D_BRANCH)),
                  _const_spec((2, D_BRANCH, D_BRANCH)), _const_spec((2, D_BRANCH)),
                  _const_spec((1, D_BRANCH)), _state_spec(h0.shape)],
        out_specs=[pl.BlockSpec((1, tb, D_BRANCH), idx), _state_spec(h0.shape)],
        scratch_shapes=[pltpu.VMEM((1, D_BRANCH), F32)],
        compiler_params=_params("parallel", "arbitrary"),
        name="rglru_scan",
    )(xb, xb, xb, conv_w, conv_b, gate_w_bd, gate_b, lam, h0)


def _rwkv_prep_kernel(s_ref, sp_ref, sn_ref, mu_ref, kk_ref, ka_ref, rk_ref, w0_ref, w2_ref,
                      a0_ref, a2_ref, ones_ref,
                      r_ref, v_ref, kkn_ref, bonus_ref, ld_ref, kt_ref, bb_ref, *, grid_shift):
    tb = TOKEN_BLOCK
    s = s_ref[0]
    width = s.shape[1]
    row = lax.broadcasted_iota(jnp.int32, (tb, width), 0)
    lane = lax.broadcasted_iota(jnp.int32, (tb, width), 1)
    if grid_shift:
        i = pl.program_id(1)
        nblk = pl.num_programs(1)
        col = row & (GRID_W - 1)
        qc = width // 4
        up_halo = sp_ref[0] * jnp.where(i > 0, 1.0, 0.0)
        dn_halo = sn_ref[0] * jnp.where(i < nblk - 1, 1.0, 0.0)
        left = jnp.where(col == 0, 0.0, pltpu.roll(s, 1, axis=0))
        right = jnp.where(col == GRID_W - 1, 0.0, pltpu.roll(s, tb - 1, axis=0))
        up = jnp.concatenate([up_halo, s[:tb - GRID_W]], axis=0)
        down = jnp.concatenate([s[GRID_W:], dn_halo], axis=0)
        sh = jnp.where(lane < qc, left,
                       jnp.where(lane < 2 * qc, right, jnp.where(lane < 3 * qc, up, down)))
    else:
        prev = jnp.where(row == 0, 0.0, pltpu.roll(s, 1, axis=0))
        nxt = jnp.where(row == tb - 1, 0.0, pltpu.roll(s, tb - 1, axis=0))
        sh = jnp.where(lane < width // 2, prev, nxt)
    s = s + mu_ref[...] * (sh - s)
    r = s[:, 0:D_BRANCH]
    k = s[:, D_BRANCH:2 * D_BRANCH]
    v = s[:, 2 * D_BRANCH:3 * D_BRANCH]
    lw = jnp.tanh(s[:, 3 * D_BRANCH:3 * D_BRANCH + LORA])
    la = s[:, 3 * D_BRANCH + LORA:3 * D_BRANCH + 2 * LORA]
    ones_bd = ones_ref[...]
    kkh = k * kk_ref[...]
    kk = kkh / jnp.maximum(jnp.sqrt(_head_sum(kkh * kkh, ones_bd)), 1e-12)
    r_ref[0] = r
    v_ref[0] = v
    kkn_ref[0] = kk
    bonus_ref[0] = _head_sum(r * k * rk_ref[...], ones_bd) * v
    for d in range(2):
        w_log = -_softplus(-(w0_ref[d:d + 1, :] + _dot(lw, w2_ref[d]))) - 0.5
        ld_ref[d, 0] = -jnp.exp(w_log)
        a = _sigmoid(a0_ref[d:d + 1, :] + _dot(la, a2_ref[d]))
        kt_ref[d, 0] = k * (1.0 + (a - 1.0) * ka_ref[...])
        bb_ref[d, 0] = kk * a


def _rwkv_prep(s, mu, k_k, k_a, r_k, w0, w2, a0, a2, ones_bd, grid_shift):
    b, n, width = s.shape
    tb = TOKEN_BLOCK
    nblk = n // tb
    assert grid_shift or nblk == 1
    halo = GRID_W
    sub = tb // halo
    nsub = n // halo
    tok = lambda bb, i: (bb, i, 0)
    tok2 = lambda bb, i: (0, bb, i, 0)
    out1 = jax.ShapeDtypeStruct((b, n, D_BRANCH), F32)
    out2 = jax.ShapeDtypeStruct((2, b, n, D_BRANCH), F32)
    return pl.pallas_call(
        functools.partial(_rwkv_prep_kernel, grid_shift=grid_shift),
        out_shape=[out1, out1, out1, out1, out2, out2, out2],
        grid=(b, nblk),
        in_specs=[pl.BlockSpec((1, tb, width), tok),
                  pl.BlockSpec((1, halo, width), lambda bb, i: (bb, jnp.maximum(i * sub - 1, 0), 0)),
                  pl.BlockSpec((1, halo, width), lambda bb, i: (bb, jnp.minimum((i + 1) * sub, nsub - 1), 0)),
                  _const_spec((1, width)), _const_spec((1, D_BRANCH)), _const_spec((1, D_BRANCH)),
                  _const_spec((1, D_BRANCH)), _const_spec((2, D_BRANCH)),
                  _const_spec((2, LORA, D_BRANCH)), _const_spec((2, D_BRANCH)),
                  _const_spec((2, LORA, D_BRANCH)), _const_spec((D_BRANCH, D_BRANCH))],
        out_specs=[pl.BlockSpec((1, tb, D_BRANCH), tok)] * 4
                  + [pl.BlockSpec((2, 1, tb, D_BRANCH), tok2)] * 3,
        compiler_params=_params("parallel", "parallel"),
        name="rwkv_prep",
    )(s, s, s, mu, k_k, k_a, r_k, w0, w2, a0, a2, ones_bd)


def _rwkv_kernel(r_ref, v_ref, kk_ref, ld_ref, kt_ref, bb_ref, s0_ref, y_ref, sf_ref, s_s, *, reverse):
    @pl.when(pl.program_id(1) == 0)
    def _():
        s_s[...] = s0_ref[0]

    L = CHUNK
    nchunk = TOKEN_BLOCK // L
    incl, strict, row, col = _tri_masks(L, reverse)
    incl_f = jnp.where(incl, 1.0, 0.0).astype(F32)
    eye_l = jnp.where(row == col, 1.0, 0.0).astype(F32)
    same16 = (row >> 4) == (col >> 4)
    same32 = (row >> 5) == (col >> 5)
    off32 = jnp.logical_and(same32, jnp.logical_not(same16))
    off64 = jnp.logical_not(same32)
    row2 = lax.broadcasted_iota(jnp.int32, (L, 2 * L), 0)
    col2 = lax.broadcasted_iota(jnp.int32, (L, 2 * L), 1)
    colm = col2 & (L - 1)
    incl2 = (colm >= row2) if reverse else (colm <= row2)
    strict_k = jnp.logical_and((colm > row2) if reverse else (colm < row2), col2 >= L)
    last = 0 if reverse else L - 1

    def chunk(ci, carry):
        cidx = (nchunk - 1 - ci) if reverse else ci
        r0 = pl.multiple_of(cidx * L, L)
        rows = pl.ds(r0, L)
        ld = ld_ref[0, 0, rows, :]
        cs = _dot_hi(incl_f, ld)
        g_in = jnp.exp(cs)
        g_inv = jnp.exp(-cs)
        kap = kk_ref[0, rows, :] * jnp.exp(cs - ld)
        bet = bb_ref[0, 0, rows, :] * g_inv
        kti = kt_ref[0, 0, rows, :] * g_inv
        rti = r_ref[0, rows, :] * g_in
        g_last = g_in[last:last + 1, :]
        vh = _split_heads(v_ref[0, rows, :])
        x2 = jnp.concatenate([_split_heads(kap), _split_heads(rti)], axis=1)
        bk = jnp.concatenate([_split_heads(bet), _split_heads(kti)], axis=1)
        s_prev = s_s[...]
        p_bk = _mm(x2, bk, _BNT, _RWKV_PREC['a'])
        p_s = _mm(x2, s_prev, _BNT, _RWKV_PREC['rest'])
        a_ub = jnp.where(strict, p_bk[:, :L, :L], 0.0)
        a_uk0 = jnp.where(strict_k, p_bk[:, :L], 0.0)
        a_r = jnp.where(incl2, p_bk[:, L:], 0.0)
        pi, pm = _RWKV_PREC['inv'], _RWKV_PREC['merge']
        pw = jnp.where(same16, -a_ub, 0.0)
        tinv = eye_l + pw
        for _ in range(3):
            pw = _mm(pw, pw, _BNN, pi)
            tinv = tinv + _mm(tinv, pw, _BNN, pi)
        for off in (off32, off64):
            tinv = tinv - _mm(tinv, _mm(jnp.where(off, a_ub, 0.0), tinv, _BNN, pm), _BNN, pm)
        zv = jnp.concatenate([jnp.zeros_like(vh), vh], axis=1)
        uh = -_mm(tinv, p_s[:, :L] + _mm(a_uk0, zv, _BNN, _RWKV_PREC['rest']), _BNN, _RWKV_PREC['solve'])
        uv = jnp.concatenate([uh, vh], axis=1)
        y = p_s[:, L:] + _mm(a_r, uv, _BNN, _RWKV_PREC['rest'])
        for h in range(N_HEADS):
            sl = slice(h * HEAD_DIM, (h + 1) * HEAD_DIM)
            y_ref[0, rows, sl] = y[h]
            upd = _mm(uv[h], bk[h], _TN, _RWKV_PREC['rest'])
            s_s[h] = (s_prev[h] + upd) * g_last[:, sl]
        return carry

    lax.fori_loop(0, nchunk, chunk, 0)

    @pl.when(pl.program_id(1) == pl.num_programs(1) - 1)
    def _():
        sf_ref[0] = s_s[...]


def _rwkv_scan(r, v, kk, ld, kt, bb, s0, direction):
    b, n, _ = r.shape
    nblk = n // TOKEN_BLOCK
    reverse = direction == 1
    idx = _blk_index(reverse, nblk)
    if reverse:
        idx2 = lambda bb_, i: (direction, bb_, nblk - 1 - i, 0)
    else:
        idx2 = lambda bb_, i: (direction, bb_, i, 0)
    tok = pl.BlockSpec((1, TOKEN_BLOCK, D_BRANCH), idx)
    tok2 = pl.BlockSpec((1, 1, TOKEN_BLOCK, D_BRANCH), idx2)
    return pl.pallas_call(
        functools.partial(_rwkv_kernel, reverse=reverse),
        out_shape=[jax.ShapeDtypeStruct((b, n, D_BRANCH), F32), jax.ShapeDtypeStruct(s0.shape, F32)],
        grid=(b, nblk),
        in_specs=[tok, tok, tok, tok2, tok2, tok2, _state_spec(s0.shape)],
        out_specs=[tok, _state_spec(s0.shape)],
        scratch_shapes=[pltpu.VMEM(s0.shape[1:], F32)],
        compiler_params=_params("parallel", "arbitrary"),
        name="rwkv_scan",
    )(r, v, kk, ld, kt, bb, s0)


def _combine_kernel(x_ref, mod_ref, maf_ref, mab_ref, og_ref, lf_ref, lb_ref, lg_ref,
                    rf_ref, rb_ref, rg_ref, wf_ref, wb_ref, bonus_ref, wg_ref, ones_ref, wo_ref,
                    *rest, final):
    if final:
        fg_ref, o_ref = rest
    else:
        (o_ref,) = rest
    ones_bd = ones_ref[...]
    og = og_ref[0]
    h_a = (maf_ref[0] + mab_ref[0]) * _sigmoid(og[:, :D_BRANCH])
    y_a = _head_norm(h_a, ones_bd) * _silu(og[:, D_BRANCH:])
    y_b = (lf_ref[0] + lb_ref[0]) * _silu(lg_ref[0])
    y_c = _head_norm(rf_ref[0] + rb_ref[0], ones_bd) * _silu(rg_ref[0])
    y_d = (_head_norm(wf_ref[0] + wb_ref[0], ones_bd) + bonus_ref[0]) * _silu(wg_ref[0])
    y = (_dot(y_a, wo_ref[0:D_BRANCH, :]) + _dot(y_b, wo_ref[D_BRANCH:2 * D_BRANCH, :])
         + _dot(y_c, wo_ref[2 * D_BRANCH:3 * D_BRANCH, :]) + _dot(y_d, wo_ref[3 * D_BRANCH:, :]))
    x = x_ref[0] + mod_ref[0, 2:3, :] * y
    if final:
        x = x * lax.rsqrt(jnp.mean(x * x, axis=-1, keepdims=True) + EPS) * fg_ref[...]
    o_ref[0] = x


def _combine(x, mod, branch_arrays, ones_bd, w_out, final_g):
    b, n, _ = x.shape
    tm = TOKEN_BLOCK
    tok = lambda i, j: (i, j, 0)
    final = final_g is not None
    in_specs = [pl.BlockSpec((1, tm, D_MODEL), tok), pl.BlockSpec((1, 3, D_MODEL), lambda i, j: (i, 0, 0))]
    in_specs += [pl.BlockSpec((1, tm, a.shape[-1]), tok) for a in branch_arrays]
    in_specs += [_const_spec((D_BRANCH, D_BRANCH)), _const_spec(w_out.shape)]
    args = [x, mod, *branch_arrays, ones_bd, w_out]
    if final:
        in_specs.append(_const_spec((1, D_MODEL)))
        args.append(final_g)
    return pl.pallas_call(
        functools.partial(_combine_kernel, final=final),
        out_shape=jax.ShapeDtypeStruct(x.shape, F32),
        grid=(b, n // tm),
        in_specs=in_specs,
        out_specs=pl.BlockSpec((1, tm, D_MODEL), tok),
        compiler_params=_params("parallel", "parallel"),
        name="combine_outproj",
    )(*args)


def _prepare_weights(w_in, mlstm_gate_b, lru_gate_w):
    p_mlstm = 5 * D_BRANCH + 4 * N_HEADS
    a0, a1 = 0, p_mlstm
    b1 = a1 + 2 * D_BRANCH
    c1 = b1 + 4 * D_BRANCH
    wa = w_in[:, :, a0:a1]
    pad = jnp.zeros((DEPTH, D_MODEL, LANES - 4 * N_HEADS), w_in.dtype)
    w_a = jnp.concatenate([wa[:, :, :3 * D_BRANCH], wa[:, :, 5 * D_BRANCH:], pad,
                           wa[:, :, 3 * D_BRANCH:5 * D_BRANCH]], axis=-1).astype(BF16)
    w_b = w_in[:, :, a1:b1].astype(BF16)
    w_c = w_in[:, :, b1:c1].astype(BF16)
    w_d = w_in[:, :, c1:].astype(BF16)
    gate_b = jnp.pad(mlstm_gate_b, ((0, 0), (0, LANES - 4 * N_HEADS)))[:, None, :]
    eye_h = jnp.eye(N_HEADS, dtype=lru_gate_w.dtype)
    gw_bd = jnp.einsum('ldghij,hk->ldghikj', lru_gate_w, eye_h).reshape(
        DEPTH, 2, 2, D_BRANCH, D_BRANCH).astype(BF16)
    return w_a, w_b, w_c, w_d, gate_b, gw_bd


def _rope_tables(n):
    rows = n // GRID_W
    row_idx = jnp.broadcast_to(jnp.arange(rows, dtype=F32)[:, None], (rows, GRID_W)).reshape(-1)
    col_idx = jnp.broadcast_to(jnp.arange(GRID_W, dtype=F32)[None, :], (rows, GRID_W)).reshape(-1)
    n_freq = HEAD_DIM // 4
    freqs = ROPE_BASE ** (-jnp.arange(n_freq, dtype=F32) / n_freq)
    ang = jnp.concatenate([row_idx[:, None] * freqs, col_idx[:, None] * freqs], -1)
    cos, sin = jnp.cos(ang), jnp.sin(ang)
    cos_t = jnp.concatenate([cos, cos] * (LANES // HEAD_DIM), axis=-1)
    sin_t = jnp.concatenate([-sin, sin] * (LANES // HEAD_DIM), axis=-1)
    return cos_t, sin_t


def _layer(x, mod, lw, states, latent, rope_tabs, ones_bd, final_g):
    c0, n0, m0, h0, r0, s0 = states
    norm_g = lw['norm_g']
    u_a, og_a = _inproj(x, norm_g, mod, lw['w_a'], (3 * D_BRANCH + LANES, 2 * D_BRANCH))
    x_b, g_b = _inproj(x, norm_g, mod, lw['w_b'], (D_BRANCH, D_BRANCH))
    u_c, g_c = _inproj(x, norm_g, mod, lw['w_c'], (3 * D_BRANCH, D_BRANCH))
    s_d, g_d = _inproj(x, norm_g, mod, lw['w_d'], (RWKV_SHIFT, D_BRANCH))

    prep = _rwkv_prep(s_d, lw['rwkv_mu'], lw['rwkv_kk'], lw['rwkv_ka'], lw['rwkv_rk'], lw['rwkv_w0'],
                      lw['rwkv_w2'], lw['rwkv_a0'], lw['rwkv_a2'], ones_bd, latent)
    r_d, v_d, kk_d, bonus_d, ld_d, kt_d, bb_d = prep

    ys, finals = [], []
    for d in range(2):
        ya, cf, nf, mf = _mlstm_scan(u_a, lw['gate_b'], c0[:, d], n0[:, d][:, :, None, :],
                                     m0[:, d][:, :, None, None], d)
        yb, hf = _lru_scan(x_b, lw['lru_conv_w'], lw['lru_conv_b'], lw['gw_bd'][d], lw['lru_gate_b'][d],
                           lw['lru_lambda'][d][None, :], h0[:, d][:, None, :], d)
        yc, rf = _ret_scan(u_c, lw['theta'], r0[:, d], d, rope_tabs)
        yd, sf = _rwkv_scan(r_d, v_d, kk_d, ld_d, kt_d, bb_d, s0[:, d], d)
        ys.append((ya, yb, yc, yd))
        finals.append((cf, nf[:, :, 0, :], mf[:, :, 0, 0], hf[:, 0, :], rf, sf))
    (yaf, ybf, ycf, ydf), (yab, ybb, ycb, ydb) = ys
    branch_arrays = [yaf, yab, og_a, ybf, ybb, g_b, ycf, ycb, g_c, ydf, ydb, bonus_d, g_d]
    x_new = _combine(x, mod, branch_arrays, ones_bd, lw['w_out'], final_g)
    new_states = tuple(jnp.stack([finals[0][j], finals[1][j]], axis=1) for j in range(6))
    return x_new, new_states


def kernel(x_prompt, x_sample, c, state_mlstm_c, state_mlstm_n, state_mlstm_m, state_lru_h, state_ret_r,
           state_rwkv_s, c_ctx, norm_g, w_mod, b_mod, w_in, w_out, mlstm_gate_b, lru_conv_w, lru_conv_b,
           lru_gate_w, lru_gate_b, lru_lambda, ret_theta, rwkv_mu, rwkv_w0, rwkv_w2, rwkv_a0, rwkv_a2,
           rwkv_kk, rwkv_ka, rwkv_rk, final_g):
    bp = x_prompt.shape[0]
    bs = x_sample.shape[0]
    assert x_prompt.shape[1] == TOKEN_BLOCK and x_sample.shape[1] % TOKEN_BLOCK == 0
    assert 1 + bs <= 8

    w_a, w_b, w_c, w_d, gate_b, gw_bd = _prepare_weights(w_in, mlstm_gate_b, lru_gate_w)
    w_out_bf = w_out.astype(BF16)
    head_of = jnp.arange(D_BRANCH) // HEAD_DIM
    ones_bd = (head_of[:, None] == head_of[None, :]).astype(BF16)
    theta = jnp.pad(ret_theta.reshape(DEPTH, 1, 2 * N_HEADS), ((0, 0), (0, 0), (0, LANES - 2 * N_HEADS)))
    rope_tabs = _rope_tables(x_sample.shape[1])

    cvec = jnp.concatenate([c_ctx[None, :], c, jnp.zeros((8 - 1 - bs, D_MODEL), F32)], axis=0)
    mods = _modulation(cvec, w_mod, b_mod).reshape(DEPTH, 8, 3, D_MODEL)

    def layer_weights(l):
        return dict(norm_g=norm_g[l][None, :], w_a=w_a[l], w_b=w_b[l], w_c=w_c[l], w_d=w_d[l],
                    gate_b=gate_b[l], gw_bd=gw_bd[l], lru_conv_w=lru_conv_w[l],
                    lru_conv_b=lru_conv_b[l][None, :], lru_gate_b=lru_gate_b[l], lru_lambda=lru_lambda[l],
                    theta=theta[l], rwkv_mu=rwkv_mu[l][None, :], rwkv_kk=rwkv_kk[l][None, :],
                    rwkv_ka=rwkv_ka[l][None, :], rwkv_rk=rwkv_rk[l][None, :], rwkv_w0=rwkv_w0[l],
                    rwkv_w2=rwkv_w2[l].astype(BF16), rwkv_a0=rwkv_a0[l], rwkv_a2=rwkv_a2[l].astype(BF16),
                    w_out=w_out_bf[l])

    zero_states = (jnp.zeros((bp, 2, N_HEADS, HEAD_DIM, HEAD_DIM), F32),
                   jnp.zeros((bp, 2, N_HEADS, HEAD_DIM), F32),
                   jnp.zeros((bp, 2, N_HEADS), F32),
                   jnp.zeros((bp, 2, D_BRANCH), F32),
                   jnp.zeros((bp, 2, N_HEADS, HEAD_DIM, HEAD_DIM), F32),
                   jnp.zeros((bp, 2, N_HEADS, HEAD_DIM, HEAD_DIM), F32))
    xp = x_prompt
    per_layer = []
    for l in range(DEPTH):
        mod = jnp.broadcast_to(mods[l, 0][None], (bp, 3, D_MODEL))
        xp, st = _layer(xp, mod, layer_weights(l), zero_states, False, None, ones_bd,
                        final_g[None, :] if l == DEPTH - 1 else None)
        per_layer.append(st)
    new_states = tuple(jnp.stack([st[j] for st in per_layer], axis=1) for j in range(6))

    xs = x_sample
    for l in range(DEPTH):
        mod = mods[l, 1:1 + bs]
        st = (state_mlstm_c[:, l], state_mlstm_n[:, l], state_mlstm_m[:, l], state_lru_h[:, l],
              state_ret_r[:, l], state_rwkv_s[:, l])
        xs, _ = _layer(xs, mod, layer_weights(l), st, True, rope_tabs, ones_bd,
                       final_g[None, :] if l == DEPTH - 1 else None)
    return (xp, xs) + new_states
```

```python
import functools

import jax
import jax.numpy as jnp
from jax import lax
from jax.experimental import pallas as pl
from jax.experimental.pallas import tpu as pltpu

F32 = jnp.float32
BF16 = jnp.bfloat16
HIGHEST = lax.Precision.HIGHEST

D_MODEL = 1024
DEPTH = 4
GRID_W = 64
D_BRANCH = 512
HEAD_DIM = 64
N_HEADS = 8
CHUNK = 64
CONV_W = 4
LRU_C = 8.0
LORA = 64
ROPE_BASE = 100.0
EPS = 1e-6
RWKV_SHIFT = 3 * D_BRANCH + 2 * LORA

LANES = 128
TOKEN_BLOCK = 256
VMEM_LIMIT = 48 * 1024 * 1024
NEG = -1e30


def _sigmoid(x):
    return 1.0 / (1.0 + jnp.exp(-x))


def _silu(x):
    return x * _sigmoid(x)


def _softplus(x):
    return jnp.maximum(x, 0.0) + jnp.log(1.0 + jnp.exp(-jnp.abs(x)))


def _log_sigmoid(x):
    return -_softplus(-x)


def _dot(a, b):
    return jnp.dot(a.astype(BF16), b.astype(BF16), preferred_element_type=F32)


def _dot_nt(a, b):
    return lax.dot_general(a.astype(BF16), b.astype(BF16), (((1,), (1,)), ((), ())),
                           preferred_element_type=F32)


def _dot_tn(a, b):
    return lax.dot_general(a.astype(BF16), b.astype(BF16), (((0,), (0,)), ((), ())),
                           preferred_element_type=F32)


def _dot_hi(a, b):
    return jnp.dot(a, b, preferred_element_type=F32, precision=HIGHEST)


def _dot_nt_hi(a, b):
    return lax.dot_general(a, b, (((1,), (1,)), ((), ())), preferred_element_type=F32,
                           precision=HIGHEST)


_NN = (((1,), (0,)), ((), ()))
_NT = (((1,), (1,)), ((), ()))
_TN = (((0,), (0,)), ((), ()))
_BNN = (((2,), (1,)), ((0,), (0,)))
_BNT = (((2,), (2,)), ((0,), (0,)))
_RWKV_PREC = dict(a=1, inv=3, merge=1, solve=3, rest=1)


def _split_heads(x):
    return jnp.stack([x[:, h * HEAD_DIM:(h + 1) * HEAD_DIM] for h in range(N_HEADS)], axis=0)


def _mm(a, b, dims, passes):
    if passes == 6:
        return lax.dot_general(a, b, dims, preferred_element_type=F32, precision=HIGHEST)
    ah = a.astype(BF16)
    bh = b.astype(BF16)
    out = lax.dot_general(ah, bh, dims, preferred_element_type=F32)
    if passes == 3:
        al = (a - ah.astype(F32)).astype(BF16)
        bl = (b - bh.astype(F32)).astype(BF16)
        out = (out + lax.dot_general(ah, bl, dims, preferred_element_type=F32)
               + lax.dot_general(al, bh, dims, preferred_element_type=F32))
    return out


def _mm_shared(lhs, b, passes):
    n, m = len(lhs), lhs[0].shape[1]
    bh = b.astype(BF16)
    hs = [x.astype(BF16) for x in lhs]
    if passes == 1:
        out = lax.dot_general(jnp.concatenate(hs, axis=1), bh, _BNN, preferred_element_type=F32)
        return [out[:, i * m:(i + 1) * m] for i in range(n)]
    assert passes == 3
    ls = [(x - h.astype(F32)).astype(BF16) for x, h in zip(lhs, hs)]
    bl = (b - bh.astype(F32)).astype(BF16)
    o1 = lax.dot_general(jnp.concatenate(hs + ls, axis=1), bh, _BNN, preferred_element_type=F32)
    o2 = lax.dot_general(jnp.concatenate(hs, axis=1), bl, _BNN, preferred_element_type=F32)
    return [o1[:, i * m:(i + 1) * m] + o1[:, (n + i) * m:(n + i + 1) * m] + o2[:, i * m:(i + 1) * m]
            for i in range(n)]


def _head_sum(x, ones_bd):
    hi = x.astype(BF16)
    lo = (x - hi.astype(F32)).astype(BF16)
    return (jnp.dot(hi, ones_bd, preferred_element_type=F32)
            + jnp.dot(lo, ones_bd, preferred_element_type=F32))


def _head_norm(x, ones_bd):
    return x * lax.rsqrt(_head_sum(x * x, ones_bd) * (1.0 / HEAD_DIM) + EPS)


def _tri_masks(n, reverse):
    row = lax.broadcasted_iota(jnp.int32, (n, n), 0)
    col = lax.broadcasted_iota(jnp.int32, (n, n), 1)
    incl = (col >= row) if reverse else (col <= row)
    strict = (col > row) if reverse else (col < row)
    return incl, strict, row, col


def _eye(n):
    row = lax.broadcasted_iota(jnp.int32, (n, n), 0)
    col = lax.broadcasted_iota(jnp.int32, (n, n), 1)
    return jnp.where(row == col, 1.0, 0.0).astype(F32)


def _params(*sem):
    return pltpu.CompilerParams(dimension_semantics=sem, vmem_limit_bytes=VMEM_LIMIT)


def _blk_index(reverse, nblk):
    if reverse:
        return lambda b, i: (b, nblk - 1 - i, 0)
    return lambda b, i: (b, i, 0)


def _state_spec(shape):
    nd = len(shape)
    return pl.BlockSpec((1,) + tuple(shape[1:]), lambda b, i: (b,) + (0,) * (nd - 1))


def _const_spec(shape):
    nd = len(shape)
    return pl.BlockSpec(tuple(shape), lambda *_: (0,) * nd)


def _mod_kernel(c_ref, w_ref, b_ref, o_ref):
    sc = _silu(c_ref[...])
    o_ref[0] = _dot(sc, w_ref[0]) + b_ref[0]


def _modulation(cvec, w_mod, b_mod):
    nt = 3
    return pl.pallas_call(
        _mod_kernel,
        out_shape=jax.ShapeDtypeStruct((DEPTH, 8, 3 * D_MODEL), F32),
        grid=(DEPTH, nt),
        in_specs=[pl.BlockSpec((8, D_MODEL), lambda l, j: (0, 0)),
                  pl.BlockSpec((1, D_MODEL, D_MODEL), lambda l, j: (l, 0, j)),
                  pl.BlockSpec((1, 1, D_MODEL), lambda l, j: (l, 0, j))],
        out_specs=pl.BlockSpec((1, 8, D_MODEL), lambda l, j: (l, 0, j)),
        compiler_params=_params("arbitrary", "arbitrary"),
        name="modulation",
    )(cvec, w_mod, b_mod.reshape(DEPTH, 1, 3 * D_MODEL))


def _inproj_kernel(x_ref, g_ref, mod_ref, w_ref, *o_refs, widths):
    x = x_ref[0]
    y = x * lax.rsqrt(jnp.mean(x * x, axis=-1, keepdims=True) + EPS) * g_ref[...]
    h = y * (1.0 + mod_ref[0, 1:2, :]) + mod_ref[0, 0:1, :]
    u = _dot(h, w_ref[...])
    off = 0
    for o_ref, w in zip(o_refs, widths):
        o_ref[0] = u[:, off:off + w]
        off += w


def _inproj(x, g, mod, w, widths):
    b, n, _ = x.shape
    p = w.shape[1]
    tm = TOKEN_BLOCK
    return pl.pallas_call(
        functools.partial(_inproj_kernel, widths=widths),
        out_shape=[jax.ShapeDtypeStruct((b, n, wd), F32) for wd in widths],
        grid=(b, n // tm),
        in_specs=[pl.BlockSpec((1, tm, D_MODEL), lambda i, j: (i, j, 0)),
                  _const_spec((1, D_MODEL)),
                  pl.BlockSpec((1, 3, D_MODEL), lambda i, j: (i, 0, 0)),
                  _const_spec((D_MODEL, p))],
        out_specs=[pl.BlockSpec((1, tm, wd), lambda i, j: (i, j, 0)) for wd in widths],
        compiler_params=_params("parallel", "parallel"),
        name="inproj",
    )(x, g, mod, w)


def _mlstm_kernel(u_ref, gb_ref, c0_ref, n0_ref, m0_ref, y_ref, cf_ref, nf_ref, mf_ref,
                  c_s, n_s, m_s, *, reverse, direction):
    @pl.when(pl.program_id(1) == 0)
    def _():
        c_s[...] = c0_ref[0]
        n_s[...] = n0_ref[0]
        m_s[...] = m0_ref[0]

    L = CHUNK
    nchunk = TOKEN_BLOCK // L
    incl, _, _, _ = _tri_masks(L, reverse)
    incl_f = jnp.where(incl, 1.0, 0.0).astype(F32)
    eye = _eye(LANES)
    ig_off = 2 * N_HEADS * direction
    fg_off = ig_off + N_HEADS
    last = 0 if reverse else L - 1

    def chunk(ci, carry):
        cidx = (nchunk - 1 - ci) if reverse else ci
        r0 = pl.multiple_of(cidx * L, L)
        rows = pl.ds(r0, L)
        gates = u_ref[0, rows, 3 * D_BRANCH:3 * D_BRANCH + LANES] + gb_ref[...]
        lf = _log_sigmoid(gates)
        bmat = _dot_hi(incl_f, lf)
        g_t = _dot_nt_hi(eye, gates)
        b_t = _dot_nt_hi(eye, bmat)
        q = _split_heads(u_ref[0, rows, 0:D_BRANCH])
        k = _split_heads(u_ref[0, rows, D_BRANCH:2 * D_BRANCH])
        v = _split_heads(u_ref[0, rows, 2 * D_BRANCH:3 * D_BRANCH])
        heads = range(N_HEADS)
        b_col = jnp.stack([bmat[:, fg_off + h:fg_off + h + 1] for h in heads], axis=0)
        ig_col = jnp.stack([gates[:, ig_off + h:ig_off + h + 1] for h in heads], axis=0)
        b_row = jnp.stack([b_t[fg_off + h:fg_off + h + 1, :] for h in heads], axis=0)
        ig_row = jnp.stack([g_t[ig_off + h:ig_off + h + 1, :] for h in heads], axis=0)
        c_prev = c_s[...]
        n_prev = n_s[...]
        m_prev = m_s[...]
        dmat = jnp.where(incl, b_col - b_row + ig_row, NEG)
        m_inter = b_col + m_prev
        m_t = jnp.maximum(m_inter, jnp.max(dmat, axis=2, keepdims=True))
        s = _mm(q, k, _BNT, 1) * (HEAD_DIM ** -0.5) * jnp.exp(dmat - m_t)
        w_inter = jnp.exp(m_inter - m_t)
        num = _mm(s, v, _BNN, 1) + w_inter * _mm(q, c_prev, _BNN, 1)
        den = (jnp.sum(s, axis=2, keepdims=True)
               + w_inter * jnp.sum(q * n_prev, axis=2, keepdims=True))
        hh = num / jnp.maximum(jnp.abs(den), jnp.exp(-m_t))
        b_last = b_col[:, last:last + 1, :]
        w_log = b_last - b_col + ig_col
        m_new = jnp.maximum(b_last + m_prev, jnp.max(w_log, axis=1, keepdims=True))
        dec = jnp.exp(b_last + m_prev - m_new)
        wk = k * (HEAD_DIM ** -0.5) * jnp.exp(w_log - m_new)
        n_s[...] = dec * n_prev + jnp.sum(wk, axis=1, keepdims=True)
        m_s[...] = m_new
        for h in heads:
            y_ref[0, rows, h * HEAD_DIM:(h + 1) * HEAD_DIM] = hh[h]
            c_s[h] = dec[h] * c_prev[h] + _dot_tn(wk[h], v[h])
        return carry

    lax.fori_loop(0, nchunk, chunk, 0)

    @pl.when(pl.program_id(1) == pl.num_programs(1) - 1)
    def _():
        cf_ref[0] = c_s[...]
        nf_ref[0] = n_s[...]
        mf_ref[0] = m_s[...]


def _mlstm_scan(u, gate_b, c0, n0, m0, direction):
    b, n, p = u.shape
    nblk = n // TOKEN_BLOCK
    reverse = direction == 1
    idx = _blk_index(reverse, nblk)
    return pl.pallas_call(
        functools.partial(_mlstm_kernel, reverse=reverse, direction=direction),
        out_shape=[jax.ShapeDtypeStruct((b, n, D_BRANCH), F32),
                   jax.ShapeDtypeStruct(c0.shape, F32),
                   jax.ShapeDtypeStruct(n0.shape, F32),
                   jax.ShapeDtypeStruct(m0.shape, F32)],
        grid=(b, nblk),
        in_specs=[pl.BlockSpec((1, TOKEN_BLOCK, p), idx),
                  _const_spec((1, LANES)),
                  _state_spec(c0.shape), _state_spec(n0.shape), _state_spec(m0.shape)],
        out_specs=[pl.BlockSpec((1, TOKEN_BLOCK, D_BRANCH), idx),
                   _state_spec(c0.shape), _state_spec(n0.shape), _state_spec(m0.shape)],
        scratch_shapes=[pltpu.VMEM(c0.shape[1:], F32), pltpu.VMEM(n0.shape[1:], F32),
                        pltpu.VMEM(m0.shape[1:], F32)],
        compiler_params=_params("parallel", "arbitrary"),
        name="mlstm_scan",
    )(u, gate_b, c0, n0, m0)


def _ret_kernel(*refs, reverse, direction, rope):
    if rope:
        u_ref, th_ref, cos_ref, sin_ref, r0_ref, y_ref, rf_ref, r_s = refs
    else:
        u_ref, th_ref, r0_ref, y_ref, rf_ref, r_s = refs

    @pl.when(pl.program_id(1) == 0)
    def _():
        r_s[...] = r0_ref[0]

    L = CHUNK
    nchunk = TOKEN_BLOCK // L
    incl, _, row, col = _tri_masks(L, reverse)
    diff = jnp.abs(row - col).astype(F32)
    pos_col = lax.broadcasted_iota(jnp.int32, (L, 1), 0).astype(F32)
    p_col = (L - 1.0 - pos_col) if reverse else pos_col
    log_g = _log_sigmoid(th_ref[...])
    lg = jnp.stack([log_g[:, N_HEADS * direction + h:N_HEADS * direction + h + 1]
                    for h in range(N_HEADS)], axis=0)
    dmat = jnp.where(incl, jnp.exp(lg * diff), 0.0)
    xi = jnp.exp(lg * (p_col + 1.0))
    wk = jnp.exp(lg * (L - 1.0 - p_col))
    dec = jnp.exp(lg * float(L))
    lane = lax.broadcasted_iota(jnp.int32, (L, D_BRANCH), 1)
    first_half = (lane & (HEAD_DIM - 1)) < (HEAD_DIM // 2)

    def rot(x, c, s):
        swapped = jnp.where(first_half, pltpu.roll(x, D_BRANCH - HEAD_DIM // 2, axis=1),
                            pltpu.roll(x, HEAD_DIM // 2, axis=1))
        return x * c + swapped * s

    def chunk(ci, carry):
        cidx = (nchunk - 1 - ci) if reverse else ci
        r0 = pl.multiple_of(cidx * L, L)
        rows = pl.ds(r0, L)
        q = u_ref[0, rows, 0:D_BRANCH]
        k = u_ref[0, rows, D_BRANCH:2 * D_BRANCH]
        if rope:
            c = jnp.concatenate([cos_ref[rows, :]] * (D_BRANCH // LANES), axis=1)
            s = jnp.concatenate([sin_ref[rows, :]] * (D_BRANCH // LANES), axis=1)
            q = rot(q, c, s)
            k = rot(k, c, s)
        qh = _split_heads(q)
        kh = _split_heads(k) * (HEAD_DIM ** -0.5)
        vh = _split_heads(u_ref[0, rows, 2 * D_BRANCH:3 * D_BRANCH])
        r_prev = r_s[...]
        s_mat = _mm(qh, kh, _BNT, 1) * dmat
        y = _mm(s_mat, vh, _BNN, 1) + xi * _mm(qh, r_prev, _BNN, 1)
        kw = kh * wk
        for h in range(N_HEADS):
            y_ref[0, rows, h * HEAD_DIM:(h + 1) * HEAD_DIM] = y[h]
            r_s[h] = dec[h] * r_prev[h] + _dot_tn(kw[h], vh[h])
        return carry

    lax.fori_loop(0, nchunk, chunk, 0)

    @pl.when(pl.program_id(1) == pl.num_programs(1) - 1)
    def _():
        rf_ref[0] = r_s[...]


def _ret_scan(u, theta, r0, direction, rope_tabs):
    b, n, p = u.shape
    nblk = n // TOKEN_BLOCK
    reverse = direction == 1
    idx = _blk_index(reverse, nblk)
    rope = rope_tabs is not None
    in_specs = [pl.BlockSpec((1, TOKEN_BLOCK, p), idx), _const_spec((1, LANES))]
    args = [u, theta]
    if rope:
        tab_idx = (lambda bb, i: (nblk - 1 - i, 0)) if reverse else (lambda bb, i: (i, 0))
        in_specs += [pl.BlockSpec((TOKEN_BLOCK, LANES), tab_idx)] * 2
        args += list(rope_tabs)
    in_specs.append(_state_spec(r0.shape))
    args.append(r0)
    return pl.pallas_call(
        functools.partial(_ret_kernel, reverse=reverse, direction=direction, rope=rope),
        out_shape=[jax.ShapeDtypeStruct((b, n, D_BRANCH), F32), jax.ShapeDtypeStruct(r0.shape, F32)],
        grid=(b, nblk),
        in_specs=in_specs,
        out_specs=[pl.BlockSpec((1, TOKEN_BLOCK, D_BRANCH), idx), _state_spec(r0.shape)],
        scratch_shapes=[pltpu.VMEM(r0.shape[1:], F32)],
        compiler_params=_params("parallel", "arbitrary"),
        name="retention_scan",
    )(*args)


def _lru_kernel(x_ref, xp_ref, xn_ref, cw_ref, cb_ref, gw_ref, gb_ref, lam_ref, h0_ref,
                y_ref, hf_ref, h_s, *, reverse):
    i = pl.program_id(1)
    nblk = pl.num_programs(1)
    blk = (nblk - 1 - i) if reverse else i

    @pl.when(i == 0)
    def _():
        h_s[...] = h0_ref[0]

    tb = TOKEN_BLOCK
    x = x_ref[0]
    row = lax.broadcasted_iota(jnp.int32, (tb, D_BRANCH), 0)
    prev_ok = jnp.where(blk > 0, 1.0, 0.0)
    next_ok = jnp.where(blk < nblk - 1, 1.0, 0.0)
    p_last = xp_ref[0, 7:8, :] * prev_ok
    n_0 = xn_ref[0, 0:1, :] * next_ok
    n_1 = xn_ref[0, 1:2, :] * next_ok
    xm1 = jnp.where(row == 0, p_last, pltpu.roll(x, 1, axis=0))
    xp1 = jnp.where(row == tb - 1, n_0, pltpu.roll(x, tb - 1, axis=0))
    xp2 = jnp.where(row == tb - 1, n_1, jnp.where(row == tb - 2, n_0, pltpu.roll(x, tb - 2, axis=0)))
    xc = (cw_ref[0:1, :] * xm1 + cw_ref[1:2, :] * x + cw_ref[2:3, :] * xp1 + cw_ref[3:4, :] * xp2
          + cb_ref[...])
    gr = _dot(xc, gw_ref[0]) + gb_ref[0:1, :]
    gi = _dot(xc, gw_ref[1]) + gb_ref[1:2, :]
    log_a = -LRU_C * _sigmoid(gr) * _softplus(-lam_ref[...])
    a = jnp.exp(log_a)
    beta = jnp.sqrt(1.0 - jnp.exp(2.0 * log_a))
    bx = beta * _sigmoid(gi) * xc

    sh = 1
    while sh < tb:
        if reverse:
            ok = row < tb - sh
            a_sh = pltpu.roll(a, tb - sh, axis=0)
            b_sh = pltpu.roll(bx, tb - sh, axis=0)
        else:
            ok = row >= sh
            a_sh = pltpu.roll(a, sh, axis=0)
            b_sh = pltpu.roll(bx, sh, axis=0)
        bx = jnp.where(ok, a * b_sh + bx, bx)
        a = jnp.where(ok, a * a_sh, a)
        sh *= 2
    h = a * h_s[...] + bx
    y_ref[0] = h
    h_s[...] = h[0:1, :] if reverse else h[tb - 1:tb, :]

    @pl.when(i == nblk - 1)
    def _():
        hf_ref[0] = h_s[...]


def _lru_scan(xb, conv_w, conv_b, gate_w_bd, gate_b, lam, h0, direction):
    b, n, _ = xb.shape
    tb = TOKEN_BLOCK
    nblk = n // tb
    reverse = direction == 1
    idx = _blk_index(reverse, nblk)
    sub = tb // 8
    nsub = n // 8
    if reverse:
        prev_idx = lambda bb, i: (bb, jnp.maximum((nblk - 1 - i) * sub - 1, 0), 0)
        next_idx = lambda bb, i: (bb, jnp.minimum((nblk - i) * sub, nsub - 1), 0)
    else:
        prev_idx = lambda bb, i: (bb, jnp.maximum(i * sub - 1, 0), 0)
        next_idx = lambda bb, i: (bb, jnp.minimum((i + 1) * sub, nsub - 1), 0)
    return pl.pallas_call(
        functools.partial(_lru_kernel, reverse=reverse),
        out_shape=[jax.ShapeDtypeStruct((b, n, D_BRANCH), F32), jax.ShapeDtypeStruct(h0.shape, F32)],
        grid=(b, nblk),
        in_specs=[pl.BlockSpec((1, tb, D_BRANCH), idx),
                  pl.BlockSpec((1, 8, D_BRANCH), prev_idx),
                  pl.BlockSpec((1, 8, D_BRANCH), next_idx),
                  _const_spec((CONV_W, D_BRANCH)), _const_spec((1, D_BRANCH)),
                  _const_spec((2, D_BRANCH, D_BRANCH)), _const_spec((2, D_BRANCH)),
                  _const_spec((1, D_BRANCH)), _state_spec(h0.shape)],
        out_specs=[pl.BlockSpec((1, tb, D_BRANCH), idx), _state_spec(h0.shape)],
        scratch_shapes=[pltpu.VMEM((1, D_BRANCH), F32)],
        compiler_params=_params("parallel", "arbitrary"),
        name="rglru_scan",
    )(xb, xb, xb, conv_w, conv_b, gate_w_bd, gate_b, lam, h0)


def _rwkv_prep_kernel(s_ref, sp_ref, sn_ref, mu_ref, kk_ref, ka_ref, rk_ref, w0_ref, w2_ref,
                      a0_ref, a2_ref, ones_ref,
                      r_ref, v_ref, kkn_ref, bonus_ref, ld_ref, kt_ref, bb_ref, *, grid_shift):
    tb = TOKEN_BLOCK
    s = s_ref[0]
    width = s.shape[1]
    row = lax.broadcasted_iota(jnp.int32, (tb, width), 0)
    lane = lax.broadcasted_iota(jnp.int32, (tb, width), 1)
    if grid_shift:
        i = pl.program_id(1)
        nblk = pl.num_programs(1)
        col = row & (GRID_W - 1)
        qc = width // 4
        up_halo = sp_ref[0] * jnp.where(i > 0, 1.0, 0.0)
        dn_halo = sn_ref[0] * jnp.where(i < nblk - 1, 1.0, 0.0)
        left = jnp.where(col == 0, 0.0, pltpu.roll(s, 1, axis=0))
        right = jnp.where(col == GRID_W - 1, 0.0, pltpu.roll(s, tb - 1, axis=0))
        up = jnp.concatenate([up_halo, s[:tb - GRID_W]], axis=0)
        down = jnp.concatenate([s[GRID_W:], dn_halo], axis=0)
        sh = jnp.where(lane < qc, left,
                       jnp.where(lane < 2 * qc, right, jnp.where(lane < 3 * qc, up, down)))
    else:
        prev = jnp.where(row == 0, 0.0, pltpu.roll(s, 1, axis=0))
        nxt = jnp.where(row == tb - 1, 0.0, pltpu.roll(s, tb - 1, axis=0))
        sh = jnp.where(lane < width // 2, prev, nxt)
    s = s + mu_ref[...] * (sh - s)
    r = s[:, 0:D_BRANCH]
    k = s[:, D_BRANCH:2 * D_BRANCH]
    v = s[:, 2 * D_BRANCH:3 * D_BRANCH]
    lw = jnp.tanh(s[:, 3 * D_BRANCH:3 * D_BRANCH + LORA])
    la = s[:, 3 * D_BRANCH + LORA:3 * D_BRANCH + 2 * LORA]
    ones_bd = ones_ref[...]
    kkh = k * kk_ref[...]
    kk = kkh / jnp.maximum(jnp.sqrt(_head_sum(kkh * kkh, ones_bd)), 1e-12)
    r_ref[0] = r
    v_ref[0] = v
    kkn_ref[0] = kk
    bonus_ref[0] = _head_sum(r * k * rk_ref[...], ones_bd) * v
    for d in range(2):
        w_log = -_softplus(-(w0_ref[d:d + 1, :] + _dot(lw, w2_ref[d]))) - 0.5
        ld_ref[d, 0] = -jnp.exp(w_log)
        a = _sigmoid(a0_ref[d:d + 1, :] + _dot(la, a2_ref[d]))
        kt_ref[d, 0] = k * (1.0 + (a - 1.0) * ka_ref[...])
        bb_ref[d, 0] = kk * a


def _rwkv_prep(s, mu, k_k, k_a, r_k, w0, w2, a0, a2, ones_bd, grid_shift):
    b, n, width = s.shape
    tb = TOKEN_BLOCK
    nblk = n // tb
    assert grid_shift or nblk == 1
    halo = GRID_W
    sub = tb // halo
    nsub = n // halo
    tok = lambda bb, i: (bb, i, 0)
    tok2 = lambda bb, i: (0, bb, i, 0)
    out1 = jax.ShapeDtypeStruct((b, n, D_BRANCH), F32)
    out2 = jax.ShapeDtypeStruct((2, b, n, D_BRANCH), F32)
    return pl.pallas_call(
        functools.partial(_rwkv_prep_kernel, grid_shift=grid_shift),
        out_shape=[out1, out1, out1, out1, out2, out2, out2],
        grid=(b, nblk),
        in_specs=[pl.BlockSpec((1, tb, width), tok),
                  pl.BlockSpec((1, halo, width), lambda bb, i: (bb, jnp.maximum(i * sub - 1, 0), 0)),
                  pl.BlockSpec((1, halo, width), lambda bb, i: (bb, jnp.minimum((i + 1) * sub, nsub - 1), 0)),
                  _const_spec((1, width)), _const_spec((1, D_BRANCH)), _const_spec((1, D_BRANCH)),
                  _const_spec((1, D_BRANCH)), _const_spec((2, D_BRANCH)),
                  _const_spec((2, LORA, D_BRANCH)), _const_spec((2, D_BRANCH)),
                  _const_spec((2, LORA, D_BRANCH)), _const_spec((D_BRANCH, D_BRANCH))],
        out_specs=[pl.BlockSpec((1, tb, D_BRANCH), tok)] * 4
                  + [pl.BlockSpec((2, 1, tb, D_BRANCH), tok2)] * 3,
        compiler_params=_params("parallel", "parallel"),
        name="rwkv_prep",
    )(s, s, s, mu, k_k, k_a, r_k, w0, w2, a0, a2, ones_bd)


def _pair_split(x):
    return jnp.stack([x[:, p * LANES:(p + 1) * LANES] for p in range(D_BRANCH // LANES)], axis=0)


def _pair_blockdiag(p, left):
    return jnp.concatenate([jnp.where(left, p, 0.0), jnp.where(left, 0.0, p)], axis=1)


def _rwkv_kernel(rf_ref, vf_ref, kkf_ref, ldf_ref, ktf_ref, bbf_ref,
                 rb_ref, vb_ref, kkb_ref, ldb_ref, ktb_ref, bbb_ref, s0_ref,
                 yf_ref, yb_ref, sf_ref, s_s):
    @pl.when(pl.program_id(1) == 0)
    def _():
        s_s[...] = s0_ref[0]

    L = CHUNK
    nchunk = TOKEN_BLOCK // L
    npair = D_BRANCH // LANES
    row = lax.broadcasted_iota(jnp.int32, (L, LANES), 0)
    lane = lax.broadcasted_iota(jnp.int32, (L, LANES), 1)
    col = lane & (HEAD_DIM - 1)
    left = lane < HEAD_DIM

    def both(fwd, bwd):
        return jnp.concatenate([jnp.broadcast_to(fwd, (npair, L, LANES)),
                                jnp.broadcast_to(bwd, (npair, L, LANES))], axis=0)

    incl = both(col <= row, col >= row)
    strict = both(col < row, col > row)
    eye_p = jnp.where(col == row, 1.0, 0.0).astype(F32)
    same16 = (row >> 4) == (col >> 4)
    same32 = (row >> 5) == (col >> 5)
    off32 = jnp.logical_and(same32, jnp.logical_not(same16))
    off64 = jnp.logical_not(same32)
    row_l = lax.broadcasted_iota(jnp.int32, (L, L), 0)
    col_l = lax.broadcasted_iota(jnp.int32, (L, L), 1)
    tri = (jnp.where(col_l <= row_l, 1.0, 0.0).astype(F32), jnp.where(col_l >= row_l, 1.0, 0.0).astype(F32))
    dir_refs = ((rf_ref, vf_ref, kkf_ref, ldf_ref, ktf_ref, bbf_ref),
                (rb_ref, vb_ref, kkb_ref, ldb_ref, ktb_ref, bbb_ref))
    pa, pi, pm, ps, pr = (_RWKV_PREC[k] for k in ('a', 'inv', 'merge', 'solve', 'rest'))

    def pair_mm(xs, p, passes):
        return _mm_shared(xs, _pair_blockdiag(p, left), passes)

    def chunk(ci, carry):
        rows = (pl.ds(pl.multiple_of(ci * L, L), L), pl.ds(pl.multiple_of((nchunk - 1 - ci) * L, L), L))
        kap, bet, kti, rti, vp, g_last = [], [], [], [], [], []
        for d in range(2):
            r_ref, v_ref, kk_ref, ld_ref, kt_ref, bb_ref = dir_refs[d]
            ld = ld_ref[0, 0, rows[d], :]
            cs = _dot_hi(tri[d], ld)
            g_in = jnp.exp(cs)
            g_inv = jnp.exp(-cs)
            kap.append(_pair_split(kk_ref[0, rows[d], :] * jnp.exp(cs - ld)))
            bet.append(_pair_split(bb_ref[0, 0, rows[d], :] * g_inv))
            kti.append(_pair_split(kt_ref[0, 0, rows[d], :] * g_inv))
            rti.append(_pair_split(r_ref[0, rows[d], :] * g_in))
            vp.append(_pair_split(v_ref[0, rows[d], :]))
            last = 0 if d == 1 else L - 1
            g_last.append(_pair_split(g_in[last:last + 1, :]))
        kap, bet, kti, rti, vp, g_last = (jnp.concatenate(t, axis=0) for t in (kap, bet, kti, rti, vp, g_last))
        s_prev = s_s[...]
        x2 = jnp.concatenate([kap, rti], axis=1)
        a_b = _mm(x2, _pair_blockdiag(bet, left), _BNT, pa)
        a_k = _mm(x2, _pair_blockdiag(kti, left), _BNT, pa)
        p_s = _mm(x2, _pair_blockdiag(s_prev, left), _BNT, pr)
        a_ub = jnp.where(strict, a_b[:, :L], 0.0)
        a_uk = jnp.where(strict, a_k[:, :L], 0.0)
        a_rb = jnp.where(incl, a_b[:, L:], 0.0)
        a_rk = jnp.where(incl, a_k[:, L:], 0.0)
        nil = jnp.where(same16, -a_ub, 0.0)
        tinv = eye_p + nil
        (pw,) = pair_mm([nil], nil, pi)
        for _ in range(2):
            t_pw, pw2 = pair_mm([tinv, pw], pw, pi)
            tinv, pw = tinv + t_pw, pw2
        tinv = tinv + pair_mm([tinv], pw, pi)[0]
        for off in (off32, off64):
            (a_t,) = pair_mm([jnp.where(off, a_ub, 0.0)], tinv, pm)
            tinv = tinv - pair_mm([tinv], a_t, pm)[0]
        (a_v,) = pair_mm([a_uk], vp, pr)
        uh = -pair_mm([tinv], p_s[:, :L] + a_v, ps)[0]
        a_r = jnp.concatenate([a_rb, a_rk], axis=2)
        uv_bd = jnp.concatenate([_pair_blockdiag(uh, left), _pair_blockdiag(vp, left)], axis=1)
        y = p_s[:, L:] + _mm(a_r, uv_bd, _BNN, pr)
        uv = jnp.concatenate([uh, vp], axis=1)
        bk = jnp.concatenate([bet, kti], axis=1)
        for g in range(2 * npair):
            upd = _mm(uv[g], bk[g], _TN, pr)
            s_s[g] = (s_prev[g] + jnp.where(left, upd[:HEAD_DIM], upd[HEAD_DIM:])) * g_last[g]
        for p in range(npair):
            yf_ref[0, rows[0], p * LANES:(p + 1) * LANES] = y[p]
            yb_ref[0, rows[1], p * LANES:(p + 1) * LANES] = y[npair + p]
        return carry

    lax.fori_loop(0, nchunk, chunk, 0)

    @pl.when(pl.program_id(1) == pl.num_programs(1) - 1)
    def _():
        sf_ref[0] = s_s[...]


def _rwkv_scan(r, v, kk, ld, kt, bb, s0):
    b, n, _ = r.shape
    nblk = n // TOKEN_BLOCK
    npair = D_BRANCH // LANES
    s0p = s0.reshape(b, 2 * npair, 2, HEAD_DIM, HEAD_DIM).transpose(0, 1, 3, 2, 4).reshape(
        b, 2 * npair, HEAD_DIM, LANES)
    fwd = lambda bb_, i: (bb_, i, 0)
    bwd = lambda bb_, i: (bb_, nblk - 1 - i, 0)
    fwd2 = lambda bb_, i: (0, bb_, i, 0)
    bwd2 = lambda bb_, i: (1, bb_, nblk - 1 - i, 0)
    tok = lambda im: pl.BlockSpec((1, TOKEN_BLOCK, D_BRANCH), im)
    tok2 = lambda im: pl.BlockSpec((1, 1, TOKEN_BLOCK, D_BRANCH), im)
    yf, yb, sfp = pl.pallas_call(
        _rwkv_kernel,
        out_shape=[jax.ShapeDtypeStruct((b, n, D_BRANCH), F32), jax.ShapeDtypeStruct((b, n, D_BRANCH), F32),
                   jax.ShapeDtypeStruct(s0p.shape, F32)],
        grid=(b, nblk),
        in_specs=[tok(fwd), tok(fwd), tok(fwd), tok2(fwd2), tok2(fwd2), tok2(fwd2),
                  tok(bwd), tok(bwd), tok(bwd), tok2(bwd2), tok2(bwd2), tok2(bwd2), _state_spec(s0p.shape)],
        out_specs=[tok(fwd), tok(bwd), _state_spec(s0p.shape)],
        scratch_shapes=[pltpu.VMEM(s0p.shape[1:], F32)],
        compiler_params=_params("parallel", "arbitrary"),
        name="rwkv_scan",
    )(r, v, kk, ld, kt, bb, r, v, kk, ld, kt, bb, s0p)
    sf = sfp.reshape(b, 2 * npair, HEAD_DIM, 2, HEAD_DIM).transpose(0, 1, 3, 2, 4).reshape(s0.shape)
    return yf, yb, sf


def _combine_kernel(x_ref, mod_ref, maf_ref, mab_ref, og_ref, lf_ref, lb_ref, lg_ref,
                    rf_ref, rb_ref, rg_ref, wf_ref, wb_ref, bonus_ref, wg_ref, ones_ref, wo_ref,
                    *rest, final):
    if final:
        fg_ref, o_ref = rest
    else:
        (o_ref,) = rest
    ones_bd = ones_ref[...]
    og = og_ref[0]
    h_a = (maf_ref[0] + mab_ref[0]) * _sigmoid(og[:, :D_BRANCH])
    y_a = _head_norm(h_a, ones_bd) * _silu(og[:, D_BRANCH:])
    y_b = (lf_ref[0] + lb_ref[0]) * _silu(lg_ref[0])
    y_c = _head_norm(rf_ref[0] + rb_ref[0], ones_bd) * _silu(rg_ref[0])
    y_d = (_head_norm(wf_ref[0] + wb_ref[0], ones_bd) + bonus_ref[0]) * _silu(wg_ref[0])
    y = (_dot(y_a, wo_ref[0:D_BRANCH, :]) + _dot(y_b, wo_ref[D_BRANCH:2 * D_BRANCH, :])
         + _dot(y_c, wo_ref[2 * D_BRANCH:3 * D_BRANCH, :]) + _dot(y_d, wo_ref[3 * D_BRANCH:, :]))
    x = x_ref[0] + mod_ref[0, 2:3, :] * y
    if final:
        x = x * lax.rsqrt(jnp.mean(x * x, axis=-1, keepdims=True) + EPS) * fg_ref[...]
    o_ref[0] = x


def _combine(x, mod, branch_arrays, ones_bd, w_out, final_g):
    b, n, _ = x.shape
    tm = TOKEN_BLOCK
    tok = lambda i, j: (i, j, 0)
    final = final_g is not None
    in_specs = [pl.BlockSpec((1, tm, D_MODEL), tok), pl.BlockSpec((1, 3, D_MODEL), lambda i, j: (i, 0, 0))]
    in_specs += [pl.BlockSpec((1, tm, a.shape[-1]), tok) for a in branch_arrays]
    in_specs += [_const_spec((D_BRANCH, D_BRANCH)), _const_spec(w_out.shape)]
    args = [x, mod, *branch_arrays, ones_bd, w_out]
    if final:
        in_specs.append(_const_spec((1, D_MODEL)))
        args.append(final_g)
    return pl.pallas_call(
        functools.partial(_combine_kernel, final=final),
        out_shape=jax.ShapeDtypeStruct(x.shape, F32),
        grid=(b, n // tm),
        in_specs=in_specs,
        out_specs=pl.BlockSpec((1, tm, D_MODEL), tok),
        compiler_params=_params("parallel", "parallel"),
        name="combine_outproj",
    )(*args)


def _prepare_weights(w_in, mlstm_gate_b, lru_gate_w):
    p_mlstm = 5 * D_BRANCH + 4 * N_HEADS
    a0, a1 = 0, p_mlstm
    b1 = a1 + 2 * D_BRANCH
    c1 = b1 + 4 * D_BRANCH
    wa = w_in[:, :, a0:a1]
    pad = jnp.zeros((DEPTH, D_MODEL, LANES - 4 * N_HEADS), w_in.dtype)
    w_a = jnp.concatenate([wa[:, :, :3 * D_BRANCH], wa[:, :, 5 * D_BRANCH:], pad,
                           wa[:, :, 3 * D_BRANCH:5 * D_BRANCH]], axis=-1).astype(BF16)
    w_b = w_in[:, :, a1:b1].astype(BF16)
    w_c = w_in[:, :, b1:c1].astype(BF16)
    w_d = w_in[:, :, c1:].astype(BF16)
    gate_b = jnp.pad(mlstm_gate_b, ((0, 0), (0, LANES - 4 * N_HEADS)))[:, None, :]
    eye_h = jnp.eye(N_HEADS, dtype=lru_gate_w.dtype)
    gw_bd = jnp.einsum('ldghij,hk->ldghikj', lru_gate_w, eye_h).reshape(
        DEPTH, 2, 2, D_BRANCH, D_BRANCH).astype(BF16)
    return w_a, w_b, w_c, w_d, gate_b, gw_bd


def _rope_tables(n):
    rows = n // GRID_W
    row_idx = jnp.broadcast_to(jnp.arange(rows, dtype=F32)[:, None], (rows, GRID_W)).reshape(-1)
    col_idx = jnp.broadcast_to(jnp.arange(GRID_W, dtype=F32)[None, :], (rows, GRID_W)).reshape(-1)
    n_freq = HEAD_DIM // 4
    freqs = ROPE_BASE ** (-jnp.arange(n_freq, dtype=F32) / n_freq)
    ang = jnp.concatenate([row_idx[:, None] * freqs, col_idx[:, None] * freqs], -1)
    cos, sin = jnp.cos(ang), jnp.sin(ang)
    cos_t = jnp.concatenate([cos, cos] * (LANES // HEAD_DIM), axis=-1)
    sin_t = jnp.concatenate([-sin, sin] * (LANES // HEAD_DIM), axis=-1)
    return cos_t, sin_t


def _layer(x, mod, lw, states, latent, rope_tabs, ones_bd, final_g):
    c0, n0, m0, h0, r0, s0 = states
    norm_g = lw['norm_g']
    u_a, og_a = _inproj(x, norm_g, mod, lw['w_a'], (3 * D_BRANCH + LANES, 2 * D_BRANCH))
    x_b, g_b = _inproj(x, norm_g, mod, lw['w_b'], (D_BRANCH, D_BRANCH))
    u_c, g_c = _inproj(x, norm_g, mod, lw['w_c'], (3 * D_BRANCH, D_BRANCH))
    s_d, g_d = _inproj(x, norm_g, mod, lw['w_d'], (RWKV_SHIFT, D_BRANCH))

    prep = _rwkv_prep(s_d, lw['rwkv_mu'], lw['rwkv_kk'], lw['rwkv_ka'], lw['rwkv_rk'], lw['rwkv_w0'],
                      lw['rwkv_w2'], lw['rwkv_a0'], lw['rwkv_a2'], ones_bd, latent)
    r_d, v_d, kk_d, bonus_d, ld_d, kt_d, bb_d = prep

    ys, finals = [], []
    for d in range(2):
        ya, cf, nf, mf = _mlstm_scan(u_a, lw['gate_b'], c0[:, d], n0[:, d][:, :, None, :],
                                     m0[:, d][:, :, None, None], d)
        yb, hf = _lru_scan(x_b, lw['lru_conv_w'], lw['lru_conv_b'], lw['gw_bd'][d], lw['lru_gate_b'][d],
                           lw['lru_lambda'][d][None, :], h0[:, d][:, None, :], d)
        yc, rf = _ret_scan(u_c, lw['theta'], r0[:, d], d, rope_tabs)
        ys.append((ya, yb, yc))
        finals.append((cf, nf[:, :, 0, :], mf[:, :, 0, 0], hf[:, 0, :], rf))
    ydf, ydb, s_new = _rwkv_scan(r_d, v_d, kk_d, ld_d, kt_d, bb_d, s0)
    (yaf, ybf, ycf), (yab, ybb, ycb) = ys
    branch_arrays = [yaf, yab, og_a, ybf, ybb, g_b, ycf, ycb, g_c, ydf, ydb, bonus_d, g_d]
    x_new = _combine(x, mod, branch_arrays, ones_bd, lw['w_out'], final_g)
    new_states = tuple(jnp.stack([finals[0][j], finals[1][j]], axis=1) for j in range(5)) + (s_new,)
    return x_new, new_states


def kernel(x_prompt, x_sample, c, state_mlstm_c, state_mlstm_n, state_mlstm_m, state_lru_h, state_ret_r,
           state_rwkv_s, c_ctx, norm_g, w_mod, b_mod, w_in, w_out, mlstm_gate_b, lru_conv_w, lru_conv_b,
           lru_gate_w, lru_gate_b, lru_lambda, ret_theta, rwkv_mu, rwkv_w0, rwkv_w2, rwkv_a0, rwkv_a2,
           rwkv_kk, rwkv_ka, rwkv_rk, final_g):
    bp = x_prompt.shape[0]
    bs = x_sample.shape[0]
    assert x_prompt.shape[1] == TOKEN_BLOCK and x_sample.shape[1] % TOKEN_BLOCK == 0
    assert 1 + bs <= 8

    w_a, w_b, w_c, w_d, gate_b, gw_bd = _prepare_weights(w_in, mlstm_gate_b, lru_gate_w)
    w_out_bf = w_out.astype(BF16)
    head_of = jnp.arange(D_BRANCH) // HEAD_DIM
    ones_bd = (head_of[:, None] == head_of[None, :]).astype(BF16)
    theta = jnp.pad(ret_theta.reshape(DEPTH, 1, 2 * N_HEADS), ((0, 0), (0, 0), (0, LANES - 2 * N_HEADS)))
    rope_tabs = _rope_tables(x_sample.shape[1])

    cvec = jnp.concatenate([c_ctx[None, :], c, jnp.zeros((8 - 1 - bs, D_MODEL), F32)], axis=0)
    mods = _modulation(cvec, w_mod, b_mod).reshape(DEPTH, 8, 3, D_MODEL)

    def layer_weights(l):
        return dict(norm_g=norm_g[l][None, :], w_a=w_a[l], w_b=w_b[l], w_c=w_c[l], w_d=w_d[l],
                    gate_b=gate_b[l], gw_bd=gw_bd[l], lru_conv_w=lru_conv_w[l],
                    lru_conv_b=lru_conv_b[l][None, :], lru_gate_b=lru_gate_b[l], lru_lambda=lru_lambda[l],
                    theta=theta[l], rwkv_mu=rwkv_mu[l][None, :], rwkv_kk=rwkv_kk[l][None, :],
                    rwkv_ka=rwkv_ka[l][None, :], rwkv_rk=rwkv_rk[l][None, :], rwkv_w0=rwkv_w0[l],
                    rwkv_w2=rwkv_w2[l].astype(BF16), rwkv_a0=rwkv_a0[l], rwkv_a2=rwkv_a2[l].astype(BF16),
                    w_out=w_out_bf[l])

    zero_states = (jnp.zeros((bp, 2, N_HEADS, HEAD_DIM, HEAD_DIM), F32),
                   jnp.zeros((bp, 2, N_HEADS, HEAD_DIM), F32),
                   jnp.zeros((bp, 2, N_HEADS), F32),
                   jnp.zeros((bp, 2, D_BRANCH), F32),
                   jnp.zeros((bp, 2, N_HEADS, HEAD_DIM, HEAD_DIM), F32),
                   jnp.zeros((bp, 2, N_HEADS, HEAD_DIM, HEAD_DIM), F32))
    xp = x_prompt
    per_layer = []
    for l in range(DEPTH):
        mod = jnp.broadcast_to(mods[l, 0][None], (bp, 3, D_MODEL))
        xp, st = _layer(xp, mod, layer_weights(l), zero_states, False, None, ones_bd,
                        final_g[None, :] if l == DEPTH - 1 else None)
        per_layer.append(st)
    new_states = tuple(jnp.stack([st[j] for st in per_layer], axis=1) for j in range(6))

    xs = x_sample
    for l in range(DEPTH):
        mod = mods[l, 1:1 + bs]
        st = (state_mlstm_c[:, l], state_mlstm_n[:, l], state_mlstm_m[:, l], state_lru_h[:, l],
              state_ret_r[:, l], state_rwkv_s[:, l])
        xs, _ = _layer(xs, mod, layer_weights(l), st, True, rope_tabs, ones_bd,
                       final_g[None, :] if l == DEPTH - 1 else None)
    return (xp, xs) + new_states
```

```python
import functools

import jax
import jax.numpy as jnp
from jax import lax
from jax.experimental import pallas as pl
from jax.experimental.pallas import tpu as pltpu

F32 = jnp.float32
BF16 = jnp.bfloat16
HIGHEST = lax.Precision.HIGHEST

D_MODEL = 1024
DEPTH = 4
GRID_W = 64
D_BRANCH = 512
HEAD_DIM = 64
N_HEADS = 8
CHUNK = 64
CONV_W = 4
LRU_C = 8.0
LORA = 64
ROPE_BASE = 100.0
EPS = 1e-6
RWKV_SHIFT = 3 * D_BRANCH + 2 * LORA

LANES = 128
TOKEN_BLOCK = 256
VMEM_LIMIT = 48 * 1024 * 1024
NEG = -1e30


def _sigmoid(x):
    return 1.0 / (1.0 + jnp.exp(-x))


def _silu(x):
    return x * _sigmoid(x)


def _softplus(x):
    return jnp.maximum(x, 0.0) + jnp.log(1.0 + jnp.exp(-jnp.abs(x)))


def _log_sigmoid(x):
    return -_softplus(-x)


def _dot(a, b):
    return jnp.dot(a.astype(BF16), b.astype(BF16), preferred_element_type=F32)


def _dot_nt(a, b):
    return lax.dot_general(a.astype(BF16), b.astype(BF16), (((1,), (1,)), ((), ())),
                           preferred_element_type=F32)


def _dot_tn(a, b):
    return lax.dot_general(a.astype(BF16), b.astype(BF16), (((0,), (0,)), ((), ())),
                           preferred_element_type=F32)


def _dot_hi(a, b):
    return jnp.dot(a, b, preferred_element_type=F32, precision=HIGHEST)


def _dot_nt_hi(a, b):
    return lax.dot_general(a, b, (((1,), (1,)), ((), ())), preferred_element_type=F32,
                           precision=HIGHEST)


_NN = (((1,), (0,)), ((), ()))
_NT = (((1,), (1,)), ((), ()))
_TN = (((0,), (0,)), ((), ()))
_BNN = (((2,), (1,)), ((0,), (0,)))
_BNT = (((2,), (2,)), ((0,), (0,)))
_RWKV_PREC = dict(a=1, inv=3, merge=1, solve=3, rest=1)


def _split_heads(x):
    return jnp.stack([x[:, h * HEAD_DIM:(h + 1) * HEAD_DIM] for h in range(N_HEADS)], axis=0)


def _mm(a, b, dims, passes):
    if passes == 6:
        return lax.dot_general(a, b, dims, preferred_element_type=F32, precision=HIGHEST)
    ah = a.astype(BF16)
    bh = b.astype(BF16)
    out = lax.dot_general(ah, bh, dims, preferred_element_type=F32)
    if passes == 3:
        al = (a - ah.astype(F32)).astype(BF16)
        bl = (b - bh.astype(F32)).astype(BF16)
        out = (out + lax.dot_general(ah, bl, dims, preferred_element_type=F32)
               + lax.dot_general(al, bh, dims, preferred_element_type=F32))
    return out


def _mm_shared(lhs, b, passes):
    n, m = len(lhs), lhs[0].shape[1]
    bh = b.astype(BF16)
    hs = [x.astype(BF16) for x in lhs]
    if passes == 1:
        out = lax.dot_general(jnp.concatenate(hs, axis=1), bh, _BNN, preferred_element_type=F32)
        return [out[:, i * m:(i + 1) * m] for i in range(n)]
    assert passes == 3
    ls = [(x - h.astype(F32)).astype(BF16) for x, h in zip(lhs, hs)]
    bl = (b - bh.astype(F32)).astype(BF16)
    o1 = lax.dot_general(jnp.concatenate(hs + ls, axis=1), bh, _BNN, preferred_element_type=F32)
    o2 = lax.dot_general(jnp.concatenate(hs, axis=1), bl, _BNN, preferred_element_type=F32)
    return [o1[:, i * m:(i + 1) * m] + o1[:, (n + i) * m:(n + i + 1) * m] + o2[:, i * m:(i + 1) * m]
            for i in range(n)]


def _head_sum(x, ones_bd):
    hi = x.astype(BF16)
    lo = (x - hi.astype(F32)).astype(BF16)
    return (jnp.dot(hi, ones_bd, preferred_element_type=F32)
            + jnp.dot(lo, ones_bd, preferred_element_type=F32))


def _head_norm(x, ones_bd):
    return x * lax.rsqrt(_head_sum(x * x, ones_bd) * (1.0 / HEAD_DIM) + EPS)


def _tri_masks(n, reverse):
    row = lax.broadcasted_iota(jnp.int32, (n, n), 0)
    col = lax.broadcasted_iota(jnp.int32, (n, n), 1)
    incl = (col >= row) if reverse else (col <= row)
    strict = (col > row) if reverse else (col < row)
    return incl, strict, row, col


def _eye(n):
    row = lax.broadcasted_iota(jnp.int32, (n, n), 0)
    col = lax.broadcasted_iota(jnp.int32, (n, n), 1)
    return jnp.where(row == col, 1.0, 0.0).astype(F32)


def _params(*sem):
    return pltpu.CompilerParams(dimension_semantics=sem, vmem_limit_bytes=VMEM_LIMIT)


def _blk_index(reverse, nblk):
    if reverse:
        return lambda b, i: (b, nblk - 1 - i, 0)
    return lambda b, i: (b, i, 0)


def _state_spec(shape):
    nd = len(shape)
    return pl.BlockSpec((1,) + tuple(shape[1:]), lambda b, i: (b,) + (0,) * (nd - 1))


def _const_spec(shape):
    nd = len(shape)
    return pl.BlockSpec(tuple(shape), lambda *_: (0,) * nd)


def _mod_kernel(c_ref, w_ref, b_ref, o_ref):
    sc = _silu(c_ref[...])
    o_ref[0] = _dot(sc, w_ref[0]) + b_ref[0]


def _modulation(cvec, w_mod, b_mod):
    nt = 3
    return pl.pallas_call(
        _mod_kernel,
        out_shape=jax.ShapeDtypeStruct((DEPTH, 8, 3 * D_MODEL), F32),
        grid=(DEPTH, nt),
        in_specs=[pl.BlockSpec((8, D_MODEL), lambda l, j: (0, 0)),
                  pl.BlockSpec((1, D_MODEL, D_MODEL), lambda l, j: (l, 0, j)),
                  pl.BlockSpec((1, 1, D_MODEL), lambda l, j: (l, 0, j))],
        out_specs=pl.BlockSpec((1, 8, D_MODEL), lambda l, j: (l, 0, j)),
        compiler_params=_params("arbitrary", "arbitrary"),
        name="modulation",
    )(cvec, w_mod, b_mod.reshape(DEPTH, 1, 3 * D_MODEL))


def _inproj_kernel(x_ref, g_ref, mod_ref, w_ref, *o_refs, widths):
    x = x_ref[0]
    y = x * lax.rsqrt(jnp.mean(x * x, axis=-1, keepdims=True) + EPS) * g_ref[...]
    h = y * (1.0 + mod_ref[0, 1:2, :]) + mod_ref[0, 0:1, :]
    u = _dot(h, w_ref[...])
    off = 0
    for o_ref, w in zip(o_refs, widths):
        o_ref[0] = u[:, off:off + w]
        off += w


def _inproj(x, g, mod, w, widths):
    b, n, _ = x.shape
    p = w.shape[1]
    tm = TOKEN_BLOCK
    return pl.pallas_call(
        functools.partial(_inproj_kernel, widths=widths),
        out_shape=[jax.ShapeDtypeStruct((b, n, wd), F32) for wd in widths],
        grid=(b, n // tm),
        in_specs=[pl.BlockSpec((1, tm, D_MODEL), lambda i, j: (i, j, 0)),
                  _const_spec((1, D_MODEL)),
                  pl.BlockSpec((1, 3, D_MODEL), lambda i, j: (i, 0, 0)),
                  _const_spec((D_MODEL, p))],
        out_specs=[pl.BlockSpec((1, tm, wd), lambda i, j: (i, j, 0)) for wd in widths],
        compiler_params=_params("parallel", "parallel"),
        name="inproj",
    )(x, g, mod, w)


def _expand_exact(x, sel):
    m = x.shape[0]
    hi = x.astype(BF16)
    r1 = x - hi.astype(F32)
    mid = r1.astype(BF16)
    lo = (r1 - mid.astype(F32)).astype(BF16)
    out = jnp.dot(jnp.concatenate([hi, mid, lo], axis=0), sel, preferred_element_type=F32)
    return out[:m] + out[m:2 * m] + out[2 * m:]


def _mlstm_kernel(uf_ref, ub_ref, gb_ref, c0_ref, n0_ref, m0_ref, yf_ref, yb_ref, cf_ref, nf_ref, mf_ref,
                  c_s, n_s, m_s):
    @pl.when(pl.program_id(1) == 0)
    def _():
        c_s[...] = c0_ref[0]
        n_s[...] = n0_ref[0]
        m_s[...] = m0_ref[0]

    L = CHUNK
    nchunk = TOKEN_BLOCK // L
    npair = D_BRANCH // LANES
    lane = lax.broadcasted_iota(jnp.int32, (L, LANES), 1)
    left = lane < HEAD_DIM
    row_w = lax.broadcasted_iota(jnp.int32, (L, D_BRANCH), 0)
    col_w = lax.broadcasted_iota(jnp.int32, (L, D_BRANCH), 1) & (HEAD_DIM - 1)
    incl_w = (col_w <= row_w, col_w >= row_w)
    diag_w = col_w == row_w
    row_g = lax.broadcasted_iota(jnp.int32, (L, LANES), 0)
    row_l = lax.broadcasted_iota(jnp.int32, (L, L), 0)
    col_l = lax.broadcasted_iota(jnp.int32, (L, L), 1)
    tri = (jnp.where(col_l <= row_l, 1.0, 0.0).astype(F32), jnp.where(col_l >= row_l, 1.0, 0.0).astype(F32))
    sel_r = lax.broadcasted_iota(jnp.int32, (LANES, D_BRANCH), 0)
    sel_h = lax.broadcasted_iota(jnp.int32, (LANES, D_BRANCH), 1) >> 6
    sel_ig = [jnp.where(sel_r == 2 * N_HEADS * d + sel_h, 1.0, 0.0).astype(BF16) for d in range(2)]
    sel_fg = [jnp.where(sel_r == 2 * N_HEADS * d + N_HEADS + sel_h, 1.0, 0.0).astype(BF16) for d in range(2)]
    sq_r = lax.broadcasted_iota(jnp.int32, (LANES, LANES), 0) >> 6
    sq_c = lax.broadcasted_iota(jnp.int32, (LANES, LANES), 1) >> 6
    same_head = sq_r == sq_c
    ones_bd = jnp.where(same_head, 1.0, 0.0).astype(BF16)
    u_refs = (uf_ref, ub_ref)
    scale = HEAD_DIM ** -0.5

    def chunk(ci, carry):
        rows = (pl.ds(pl.multiple_of(ci * L, L), L), pl.ds(pl.multiple_of((nchunk - 1 - ci) * L, L), L))
        parts = []
        for d in range(2):
            u_ref = u_refs[d]
            last = 0 if d == 1 else L - 1
            gates = u_ref[0, rows[d], 3 * D_BRANCH:3 * D_BRANCH + LANES] + gb_ref[...]
            bmat = _dot_hi(tri[d], _log_sigmoid(gates))
            b_at_ig = pltpu.roll(bmat, LANES - N_HEADS, axis=1)
            cmx = gates - b_at_ig
            sh = 1
            while sh < L:
                if d == 0:
                    cmx = jnp.where(row_g >= sh, jnp.maximum(cmx, pltpu.roll(cmx, sh, axis=0)), cmx)
                else:
                    cmx = jnp.where(row_g < L - sh, jnp.maximum(cmx, pltpu.roll(cmx, L - sh, axis=0)), cmx)
                sh *= 2
            rmax = _expand_exact(b_at_ig + cmx, sel_ig[d])
            bcol = _expand_exact(bmat, sel_fg[d])
            igcol = _expand_exact(gates, sel_ig[d])
            crow = jnp.sum(jnp.where(diag_w, igcol - bcol, 0.0), axis=0, keepdims=True)
            m_prev = m_s[d]
            m_inter = bcol + m_prev
            m_t = jnp.maximum(m_inter, rmax)
            pexp = jnp.exp(jnp.where(incl_w[d], bcol + crow, NEG) - m_t)
            w_inter = jnp.exp(m_inter - m_t)
            em = jnp.exp(-m_t)
            b_last = bcol[last:last + 1, :]
            w_log = b_last - bcol + igcol
            m_new = jnp.maximum(b_last + m_prev, jnp.max(w_log, axis=0, keepdims=True))
            dec = jnp.exp(b_last + m_prev - m_new)
            wdec = jnp.exp(w_log - m_new)
            m_s[d] = m_new
            q = u_ref[0, rows[d], 0:D_BRANCH]
            k = u_ref[0, rows[d], D_BRANCH:2 * D_BRANCH] * scale
            v = u_ref[0, rows[d], 2 * D_BRANCH:3 * D_BRANCH]
            parts.append([_pair_split(t) for t in (q, k, v, pexp, w_inter, em, k * wdec, dec)])
        qp, kp, vp, pexp, w_inter, em, wk, dec = (jnp.concatenate(t, axis=0) for t in zip(*parts))
        c_prev = c_s[...]
        n_prev = n_s[...]
        s = _mm(qp, _pair_blockdiag(kp, left), _BNT, 1) * pexp
        num = _mm(s, _pair_blockdiag(vp, left), _BNN, 1) + w_inter * _mm(qp, c_prev, _BNN, 1)
        both = jnp.concatenate([s, qp * n_prev], axis=1).reshape(2 * npair * 2 * L, LANES)
        sums = _head_sum(both, ones_bd).reshape(2 * npair, 2 * L, LANES)
        den = sums[:, :L] + w_inter * sums[:, L:]
        hh = num / jnp.maximum(jnp.abs(den), em)
        n_s[...] = dec * n_prev + jnp.sum(wk, axis=1, keepdims=True)
        for g in range(2 * npair):
            c_s[g] = dec[g] * c_prev[g] + jnp.where(same_head, _dot_tn(wk[g], vp[g]), 0.0)
        for p in range(npair):
            yf_ref[0, rows[0], p * LANES:(p + 1) * LANES] = hh[p]
            yb_ref[0, rows[1], p * LANES:(p + 1) * LANES] = hh[npair + p]
        return carry

    lax.fori_loop(0, nchunk, chunk, 0)

    @pl.when(pl.program_id(1) == pl.num_programs(1) - 1)
    def _():
        cf_ref[0] = c_s[...]
        nf_ref[0] = n_s[...]
        mf_ref[0] = m_s[...]


def _mlstm_scan(u, gate_b, c0, n0, m0):
    b, n, p = u.shape
    nblk = n // TOKEN_BLOCK
    groups = 2 * (D_BRANCH // LANES)
    fwd = lambda bb, i: (bb, i, 0)
    bwd = lambda bb, i: (bb, nblk - 1 - i, 0)
    c0p = _to_blockdiag(c0)
    n0p = n0.reshape(b, groups, 1, LANES)
    m0p = jnp.repeat(m0, HEAD_DIM, axis=-1)[:, :, None, :]
    states = (c0p, n0p, m0p)
    yf, yb, cfp, nfp, mfp = pl.pallas_call(
        _mlstm_kernel,
        out_shape=[jax.ShapeDtypeStruct((b, n, D_BRANCH), F32), jax.ShapeDtypeStruct((b, n, D_BRANCH), F32)]
                  + [jax.ShapeDtypeStruct(s.shape, F32) for s in states],
        grid=(b, nblk),
        in_specs=[pl.BlockSpec((1, TOKEN_BLOCK, p), fwd), pl.BlockSpec((1, TOKEN_BLOCK, p), bwd),
                  _const_spec((1, LANES))] + [_state_spec(s.shape) for s in states],
        out_specs=[pl.BlockSpec((1, TOKEN_BLOCK, D_BRANCH), fwd), pl.BlockSpec((1, TOKEN_BLOCK, D_BRANCH), bwd)]
                  + [_state_spec(s.shape) for s in states],
        scratch_shapes=[pltpu.VMEM(s.shape[1:], F32) for s in states],
        compiler_params=_params("parallel", "arbitrary"),
        name="mlstm_scan",
    )(u, u, gate_b, *states)
    return yf, yb, _from_blockdiag(cfp, c0.shape), nfp.reshape(n0.shape), mfp[:, :, 0, ::HEAD_DIM]


def _ret_kernel(*refs, rope):
    if rope:
        uf_ref, ub_ref, th_ref, cosf_ref, sinf_ref, cosb_ref, sinb_ref, r0_ref, yf_ref, yb_ref, rf_ref, r_s = refs
        tabs = ((cosf_ref, sinf_ref), (cosb_ref, sinb_ref))
    else:
        uf_ref, ub_ref, th_ref, r0_ref, yf_ref, yb_ref, rf_ref, r_s = refs
    u_refs = (uf_ref, ub_ref)

    @pl.when(pl.program_id(1) == 0)
    def _():
        r_s[...] = r0_ref[0]

    L = CHUNK
    nchunk = TOKEN_BLOCK // L
    npair = D_BRANCH // LANES
    row = lax.broadcasted_iota(jnp.int32, (L, LANES), 0)
    lane = lax.broadcasted_iota(jnp.int32, (L, LANES), 1)
    col = lane & (HEAD_DIM - 1)
    left = lane < HEAD_DIM
    left_sq = lax.broadcasted_iota(jnp.int32, (LANES, LANES), 1) < HEAD_DIM
    same_head = (lax.broadcasted_iota(jnp.int32, (LANES, LANES), 0) < HEAD_DIM) == left_sq

    def both(fwd, bwd):
        return jnp.concatenate([jnp.broadcast_to(fwd, (npair, L, LANES)),
                                jnp.broadcast_to(bwd, (npair, L, LANES))], axis=0)

    incl = both(col <= row, col >= row)
    diff = jnp.abs(row - col).astype(F32)
    rowf = row.astype(F32)
    p_pos = both(rowf, L - 1.0 - rowf)
    log_g = _log_sigmoid(th_ref[...])
    lg = jnp.stack([jnp.where(left[0:1], log_g[:, N_HEADS * d + 2 * p:N_HEADS * d + 2 * p + 1],
                              log_g[:, N_HEADS * d + 2 * p + 1:N_HEADS * d + 2 * p + 2])
                    for d in range(2) for p in range(npair)], axis=0)
    dmat = jnp.where(incl, jnp.exp(lg * diff), 0.0)
    xi = jnp.exp(lg * (p_pos + 1.0))
    wk = jnp.exp(lg * (L - 1.0 - p_pos))
    dec = jnp.exp(lg * float(L))
    lane_w = lax.broadcasted_iota(jnp.int32, (L, D_BRANCH), 1)
    first_half = (lane_w & (HEAD_DIM - 1)) < (HEAD_DIM // 2)

    def rot(x, c, s):
        swapped = jnp.where(first_half, pltpu.roll(x, D_BRANCH - HEAD_DIM // 2, axis=1),
                            pltpu.roll(x, HEAD_DIM // 2, axis=1))
        return x * c + swapped * s

    def chunk(ci, carry):
        rows = (pl.ds(pl.multiple_of(ci * L, L), L), pl.ds(pl.multiple_of((nchunk - 1 - ci) * L, L), L))
        qs, ks, vs = [], [], []
        for d in range(2):
            q = u_refs[d][0, rows[d], 0:D_BRANCH]
            k = u_refs[d][0, rows[d], D_BRANCH:2 * D_BRANCH]
            if rope:
                c = jnp.concatenate([tabs[d][0][rows[d], :]] * npair, axis=1)
                s = jnp.concatenate([tabs[d][1][rows[d], :]] * npair, axis=1)
                q = rot(q, c, s)
                k = rot(k, c, s)
            qs.append(_pair_split(q))
            ks.append(_pair_split(k * (HEAD_DIM ** -0.5)))
            vs.append(_pair_split(u_refs[d][0, rows[d], 2 * D_BRANCH:3 * D_BRANCH]))
        qp, kp, vp = (jnp.concatenate(t, axis=0) for t in (qs, ks, vs))
        r_prev = r_s[...]
        s_mat = _mm(qp, _pair_blockdiag(kp, left), _BNT, 1) * dmat
        y = _mm(s_mat, _pair_blockdiag(vp, left), _BNN, 1) + xi * _mm(qp, r_prev, _BNN, 1)
        kw = kp * wk
        for g in range(2 * npair):
            r_s[g] = dec[g] * r_prev[g] + jnp.where(same_head, _dot_tn(kw[g], vp[g]), 0.0)
        for p in range(npair):
            yf_ref[0, rows[0], p * LANES:(p + 1) * LANES] = y[p]
            yb_ref[0, rows[1], p * LANES:(p + 1) * LANES] = y[npair + p]
        return carry

    lax.fori_loop(0, nchunk, chunk, 0)

    @pl.when(pl.program_id(1) == pl.num_programs(1) - 1)
    def _():
        rf_ref[0] = r_s[...]


def _to_blockdiag(s):
    b = s.shape[0]
    g = 2 * (D_BRANCH // LANES)
    x = s.reshape(b, g, 2, HEAD_DIM, HEAD_DIM)
    return jnp.einsum('bghdv,hk->bghdkv', x, jnp.eye(2, dtype=s.dtype)).reshape(b, g, LANES, LANES)


def _from_blockdiag(sbd, shape):
    b = sbd.shape[0]
    x = sbd.reshape(b, 2 * (D_BRANCH // LANES), 2, HEAD_DIM, 2, HEAD_DIM)
    return jnp.stack([x[:, :, 0, :, 0, :], x[:, :, 1, :, 1, :]], axis=2).reshape(shape)


def _ret_scan(u, theta, r0, rope_tabs):
    b, n, p = u.shape
    nblk = n // TOKEN_BLOCK
    fwd = lambda bb, i: (bb, i, 0)
    bwd = lambda bb, i: (bb, nblk - 1 - i, 0)
    rope = rope_tabs is not None
    r0p = _to_blockdiag(r0)
    in_specs = [pl.BlockSpec((1, TOKEN_BLOCK, p), fwd), pl.BlockSpec((1, TOKEN_BLOCK, p), bwd),
                _const_spec((1, LANES))]
    args = [u, u, theta]
    if rope:
        in_specs += [pl.BlockSpec((TOKEN_BLOCK, LANES), lambda bb, i: (i, 0))] * 2
        in_specs += [pl.BlockSpec((TOKEN_BLOCK, LANES), lambda bb, i: (nblk - 1 - i, 0))] * 2
        args += list(rope_tabs) * 2
    in_specs.append(_state_spec(r0p.shape))
    args.append(r0p)
    yf, yb, rfp = pl.pallas_call(
        functools.partial(_ret_kernel, rope=rope),
        out_shape=[jax.ShapeDtypeStruct((b, n, D_BRANCH), F32), jax.ShapeDtypeStruct((b, n, D_BRANCH), F32),
                   jax.ShapeDtypeStruct(r0p.shape, F32)],
        grid=(b, nblk),
        in_specs=in_specs,
        out_specs=[pl.BlockSpec((1, TOKEN_BLOCK, D_BRANCH), fwd), pl.BlockSpec((1, TOKEN_BLOCK, D_BRANCH), bwd),
                   _state_spec(r0p.shape)],
        scratch_shapes=[pltpu.VMEM(r0p.shape[1:], F32)],
        compiler_params=_params("parallel", "arbitrary"),
        name="retention_scan",
    )(*args)
    return yf, yb, _from_blockdiag(rfp, r0.shape)


def _lru_kernel(x_ref, xp_ref, xn_ref, cw_ref, cb_ref, gw_ref, gb_ref, lam_ref, h0_ref,
                y_ref, hf_ref, h_s, *, reverse):
    i = pl.program_id(1)
    nblk = pl.num_programs(1)
    blk = (nblk - 1 - i) if reverse else i

    @pl.when(i == 0)
    def _():
        h_s[...] = h0_ref[0]

    tb = TOKEN_BLOCK
    x = x_ref[0]
    row = lax.broadcasted_iota(jnp.int32, (tb, D_BRANCH), 0)
    prev_ok = jnp.where(blk > 0, 1.0, 0.0)
    next_ok = jnp.where(blk < nblk - 1, 1.0, 0.0)
    p_last = xp_ref[0, 7:8, :] * prev_ok
    n_0 = xn_ref[0, 0:1, :] * next_ok
    n_1 = xn_ref[0, 1:2, :] * next_ok
    xm1 = jnp.where(row == 0, p_last, pltpu.roll(x, 1, axis=0))
    xp1 = jnp.where(row == tb - 1, n_0, pltpu.roll(x, tb - 1, axis=0))
    xp2 = jnp.where(row == tb - 1, n_1, jnp.where(row == tb - 2, n_0, pltpu.roll(x, tb - 2, axis=0)))
    xc = (cw_ref[0:1, :] * xm1 + cw_ref[1:2, :] * x + cw_ref[2:3, :] * xp1 + cw_ref[3:4, :] * xp2
          + cb_ref[...])
    gr = _dot(xc, gw_ref[0]) + gb_ref[0:1, :]
    gi = _dot(xc, gw_ref[1]) + gb_ref[1:2, :]
    log_a = -LRU_C * _sigmoid(gr) * _softplus(-lam_ref[...])
    a = jnp.exp(log_a)
    beta = jnp.sqrt(1.0 - jnp.exp(2.0 * log_a))
    bx = beta * _sigmoid(gi) * xc

    sh = 1
    while sh < tb:
        if reverse:
            ok = row < tb - sh
            a_sh = pltpu.roll(a, tb - sh, axis=0)
            b_sh = pltpu.roll(bx, tb - sh, axis=0)
        else:
            ok = row >= sh
            a_sh = pltpu.roll(a, sh, axis=0)
            b_sh = pltpu.roll(bx, sh, axis=0)
        bx = jnp.where(ok, a * b_sh + bx, bx)
        a = jnp.where(ok, a * a_sh, a)
        sh *= 2
    h = a * h_s[...] + bx
    y_ref[0] = h
    h_s[...] = h[0:1, :] if reverse else h[tb - 1:tb, :]

    @pl.when(i == nblk - 1)
    def _():
        hf_ref[0] = h_s[...]


def _lru_scan(xb, conv_w, conv_b, gate_w_bd, gate_b, lam, h0, direction):
    b, n, _ = xb.shape
    tb = TOKEN_BLOCK
    nblk = n // tb
    reverse = direction == 1
    idx = _blk_index(reverse, nblk)
    sub = tb // 8
    nsub = n // 8
    if reverse:
        prev_idx = lambda bb, i: (bb, jnp.maximum((nblk - 1 - i) * sub - 1, 0), 0)
        next_idx = lambda bb, i: (bb, jnp.minimum((nblk - i) * sub, nsub - 1), 0)
    else:
        prev_idx = lambda bb, i: (bb, jnp.maximum(i * sub - 1, 0), 0)
        next_idx = lambda bb, i: (bb, jnp.minimum((i + 1) * sub, nsub - 1), 0)
    return pl.pallas_call(
        functools.partial(_lru_kernel, reverse=reverse),
        out_shape=[jax.ShapeDtypeStruct((b, n, D_BRANCH), F32), jax.ShapeDtypeStruct(h0.shape, F32)],
        grid=(b, nblk),
        in_specs=[pl.BlockSpec((1, tb, D_BRANCH), idx),
                  pl.BlockSpec((1, 8, D_BRANCH), prev_idx),
                  pl.BlockSpec((1, 8, D_BRANCH), next_idx),
                  _const_spec((CONV_W, D_BRANCH)), _const_spec((1, D_BRANCH)),
                  _const_spec((2, D_BRANCH, D_BRANCH)), _const_spec((2, D_BRANCH)),
                  _const_spec((1, D_BRANCH)), _state_spec(h0.shape)],
        out_specs=[pl.BlockSpec((1, tb, D_BRANCH), idx), _state_spec(h0.shape)],
        scratch_shapes=[pltpu.VMEM((1, D_BRANCH), F32)],
        compiler_params=_params("parallel", "arbitrary"),
        name="rglru_scan",
    )(xb, xb, xb, conv_w, conv_b, gate_w_bd, gate_b, lam, h0)


def _rwkv_prep_kernel(s_ref, sp_ref, sn_ref, mu_ref, kk_ref, ka_ref, rk_ref, w0_ref, w2_ref,
                      a0_ref, a2_ref, ones_ref,
                      r_ref, v_ref, kkn_ref, bonus_ref, ld_ref, kt_ref, bb_ref, *, grid_shift):
    tb = TOKEN_BLOCK
    s = s_ref[0]
    width = s.shape[1]
    row = lax.broadcasted_iota(jnp.int32, (tb, width), 0)
    lane = lax.broadcasted_iota(jnp.int32, (tb, width), 1)
    if grid_shift:
        i = pl.program_id(1)
        nblk = pl.num_programs(1)
        col = row & (GRID_W - 1)
        qc = width // 4
        up_halo = sp_ref[0] * jnp.where(i > 0, 1.0, 0.0)
        dn_halo = sn_ref[0] * jnp.where(i < nblk - 1, 1.0, 0.0)
        left = jnp.where(col == 0, 0.0, pltpu.roll(s, 1, axis=0))
        right = jnp.where(col == GRID_W - 1, 0.0, pltpu.roll(s, tb - 1, axis=0))
        up = jnp.concatenate([up_halo, s[:tb - GRID_W]], axis=0)
        down = jnp.concatenate([s[GRID_W:], dn_halo], axis=0)
        sh = jnp.where(lane < qc, left,
                       jnp.where(lane < 2 * qc, right, jnp.where(lane < 3 * qc, up, down)))
    else:
        prev = jnp.where(row == 0, 0.0, pltpu.roll(s, 1, axis=0))
        nxt = jnp.where(row == tb - 1, 0.0, pltpu.roll(s, tb - 1, axis=0))
        sh = jnp.where(lane < width // 2, prev, nxt)
    s = s + mu_ref[...] * (sh - s)
    r = s[:, 0:D_BRANCH]
    k = s[:, D_BRANCH:2 * D_BRANCH]
    v = s[:, 2 * D_BRANCH:3 * D_BRANCH]
    lw = jnp.tanh(s[:, 3 * D_BRANCH:3 * D_BRANCH + LORA])
    la = s[:, 3 * D_BRANCH + LORA:3 * D_BRANCH + 2 * LORA]
    ones_bd = ones_ref[...]
    kkh = k * kk_ref[...]
    kk = kkh / jnp.maximum(jnp.sqrt(_head_sum(kkh * kkh, ones_bd)), 1e-12)
    r_ref[0] = r
    v_ref[0] = v
    kkn_ref[0] = kk
    bonus_ref[0] = _head_sum(r * k * rk_ref[...], ones_bd) * v
    for d in range(2):
        w_log = -_softplus(-(w0_ref[d:d + 1, :] + _dot(lw, w2_ref[d]))) - 0.5
        ld_ref[d, 0] = -jnp.exp(w_log)
        a = _sigmoid(a0_ref[d:d + 1, :] + _dot(la, a2_ref[d]))
        kt_ref[d, 0] = k * (1.0 + (a - 1.0) * ka_ref[...])
        bb_ref[d, 0] = kk * a


def _rwkv_prep(s, mu, k_k, k_a, r_k, w0, w2, a0, a2, ones_bd, grid_shift):
    b, n, width = s.shape
    tb = TOKEN_BLOCK
    nblk = n // tb
    assert grid_shift or nblk == 1
    halo = GRID_W
    sub = tb // halo
    nsub = n // halo
    tok = lambda bb, i: (bb, i, 0)
    tok2 = lambda bb, i: (0, bb, i, 0)
    out1 = jax.ShapeDtypeStruct((b, n, D_BRANCH), F32)
    out2 = jax.ShapeDtypeStruct((2, b, n, D_BRANCH), F32)
    return pl.pallas_call(
        functools.partial(_rwkv_prep_kernel, grid_shift=grid_shift),
        out_shape=[out1, out1, out1, out1, out2, out2, out2],
        grid=(b, nblk),
        in_specs=[pl.BlockSpec((1, tb, width), tok),
                  pl.BlockSpec((1, halo, width), lambda bb, i: (bb, jnp.maximum(i * sub - 1, 0), 0)),
                  pl.BlockSpec((1, halo, width), lambda bb, i: (bb, jnp.minimum((i + 1) * sub, nsub - 1), 0)),
                  _const_spec((1, width)), _const_spec((1, D_BRANCH)), _const_spec((1, D_BRANCH)),
                  _const_spec((1, D_BRANCH)), _const_spec((2, D_BRANCH)),
                  _const_spec((2, LORA, D_BRANCH)), _const_spec((2, D_BRANCH)),
                  _const_spec((2, LORA, D_BRANCH)), _const_spec((D_BRANCH, D_BRANCH))],
        out_specs=[pl.BlockSpec((1, tb, D_BRANCH), tok)] * 4
                  + [pl.BlockSpec((2, 1, tb, D_BRANCH), tok2)] * 3,
        compiler_params=_params("parallel", "parallel"),
        name="rwkv_prep",
    )(s, s, s, mu, k_k, k_a, r_k, w0, w2, a0, a2, ones_bd)


def _pair_split(x):
    return jnp.stack([x[:, p * LANES:(p + 1) * LANES] for p in range(D_BRANCH // LANES)], axis=0)


def _pair_blockdiag(p, left):
    return jnp.concatenate([jnp.where(left, p, 0.0), jnp.where(left, 0.0, p)], axis=1)


def _rwkv_kernel(rf_ref, vf_ref, kkf_ref, ldf_ref, ktf_ref, bbf_ref,
                 rb_ref, vb_ref, kkb_ref, ldb_ref, ktb_ref, bbb_ref, s0_ref,
                 yf_ref, yb_ref, sf_ref, s_s):
    @pl.when(pl.program_id(1) == 0)
    def _():
        s_s[...] = s0_ref[0]

    L = CHUNK
    nchunk = TOKEN_BLOCK // L
    npair = D_BRANCH // LANES
    row = lax.broadcasted_iota(jnp.int32, (L, LANES), 0)
    lane = lax.broadcasted_iota(jnp.int32, (L, LANES), 1)
    col = lane & (HEAD_DIM - 1)
    left = lane < HEAD_DIM

    def both(fwd, bwd):
        return jnp.concatenate([jnp.broadcast_to(fwd, (npair, L, LANES)),
                                jnp.broadcast_to(bwd, (npair, L, LANES))], axis=0)

    incl = both(col <= row, col >= row)
    strict = both(col < row, col > row)
    eye_p = jnp.where(col == row, 1.0, 0.0).astype(F32)
    same16 = (row >> 4) == (col >> 4)
    same32 = (row >> 5) == (col >> 5)
    off32 = jnp.logical_and(same32, jnp.logical_not(same16))
    off64 = jnp.logical_not(same32)
    row_l = lax.broadcasted_iota(jnp.int32, (L, L), 0)
    col_l = lax.broadcasted_iota(jnp.int32, (L, L), 1)
    tri = (jnp.where(col_l <= row_l, 1.0, 0.0).astype(F32), jnp.where(col_l >= row_l, 1.0, 0.0).astype(F32))
    dir_refs = ((rf_ref, vf_ref, kkf_ref, ldf_ref, ktf_ref, bbf_ref),
                (rb_ref, vb_ref, kkb_ref, ldb_ref, ktb_ref, bbb_ref))
    pa, pi, pm, ps, pr = (_RWKV_PREC[k] for k in ('a', 'inv', 'merge', 'solve', 'rest'))

    def pair_mm(xs, p, passes):
        return _mm_shared(xs, _pair_blockdiag(p, left), passes)

    def chunk(ci, carry):
        rows = (pl.ds(pl.multiple_of(ci * L, L), L), pl.ds(pl.multiple_of((nchunk - 1 - ci) * L, L), L))
        kap, bet, kti, rti, vp, g_last = [], [], [], [], [], []
        for d in range(2):
            r_ref, v_ref, kk_ref, ld_ref, kt_ref, bb_ref = dir_refs[d]
            ld = ld_ref[0, 0, rows[d], :]
            cs = _dot_hi(tri[d], ld)
            g_in = jnp.exp(cs)
            g_inv = jnp.exp(-cs)
            kap.append(_pair_split(kk_ref[0, rows[d], :] * jnp.exp(cs - ld)))
            bet.append(_pair_split(bb_ref[0, 0, rows[d], :] * g_inv))
            kti.append(_pair_split(kt_ref[0, 0, rows[d], :] * g_inv))
            rti.append(_pair_split(r_ref[0, rows[d], :] * g_in))
            vp.append(_pair_split(v_ref[0, rows[d], :]))
            last = 0 if d == 1 else L - 1
            g_last.append(_pair_split(g_in[last:last + 1, :]))
        kap, bet, kti, rti, vp, g_last = (jnp.concatenate(t, axis=0) for t in (kap, bet, kti, rti, vp, g_last))
        s_prev = s_s[...]
        x2 = jnp.concatenate([kap, rti], axis=1)
        a_b = _mm(x2, _pair_blockdiag(bet, left), _BNT, pa)
        a_k = _mm(x2, _pair_blockdiag(kti, left), _BNT, pa)
        p_s = _mm(x2, _pair_blockdiag(s_prev, left), _BNT, pr)
        a_ub = jnp.where(strict, a_b[:, :L], 0.0)
        a_uk = jnp.where(strict, a_k[:, :L], 0.0)
        a_rb = jnp.where(incl, a_b[:, L:], 0.0)
        a_rk = jnp.where(incl, a_k[:, L:], 0.0)
        nil = jnp.where(same16, -a_ub, 0.0)
        tinv = eye_p + nil
        (pw,) = pair_mm([nil], nil, pi)
        for _ in range(2):
            t_pw, pw2 = pair_mm([tinv, pw], pw, pi)
            tinv, pw = tinv + t_pw, pw2
        tinv = tinv + pair_mm([tinv], pw, pi)[0]
        for off in (off32, off64):
            (a_t,) = pair_mm([jnp.where(off, a_ub, 0.0)], tinv, pm)
            tinv = tinv - pair_mm([tinv], a_t, pm)[0]
        (a_v,) = pair_mm([a_uk], vp, pr)
        uh = -pair_mm([tinv], p_s[:, :L] + a_v, ps)[0]
        a_r = jnp.concatenate([a_rb, a_rk], axis=2)
        uv_bd = jnp.concatenate([_pair_blockdiag(uh, left), _pair_blockdiag(vp, left)], axis=1)
        y = p_s[:, L:] + _mm(a_r, uv_bd, _BNN, pr)
        uv = jnp.concatenate([uh, vp], axis=1)
        bk = jnp.concatenate([bet, kti], axis=1)
        for g in range(2 * npair):
            upd = _mm(uv[g], bk[g], _TN, pr)
            s_s[g] = (s_prev[g] + jnp.where(left, upd[:HEAD_DIM], upd[HEAD_DIM:])) * g_last[g]
        for p in range(npair):
            yf_ref[0, rows[0], p * LANES:(p + 1) * LANES] = y[p]
            yb_ref[0, rows[1], p * LANES:(p + 1) * LANES] = y[npair + p]
        return carry

    lax.fori_loop(0, nchunk, chunk, 0)

    @pl.when(pl.program_id(1) == pl.num_programs(1) - 1)
    def _():
        sf_ref[0] = s_s[...]


def _rwkv_scan(r, v, kk, ld, kt, bb, s0):
    b, n, _ = r.shape
    nblk = n // TOKEN_BLOCK
    npair = D_BRANCH // LANES
    s0p = s0.reshape(b, 2 * npair, 2, HEAD_DIM, HEAD_DIM).transpose(0, 1, 3, 2, 4).reshape(
        b, 2 * npair, HEAD_DIM, LANES)
    fwd = lambda bb_, i: (bb_, i, 0)
    bwd = lambda bb_, i: (bb_, nblk - 1 - i, 0)
    fwd2 = lambda bb_, i: (0, bb_, i, 0)
    bwd2 = lambda bb_, i: (1, bb_, nblk - 1 - i, 0)
    tok = lambda im: pl.BlockSpec((1, TOKEN_BLOCK, D_BRANCH), im)
    tok2 = lambda im: pl.BlockSpec((1, 1, TOKEN_BLOCK, D_BRANCH), im)
    yf, yb, sfp = pl.pallas_call(
        _rwkv_kernel,
        out_shape=[jax.ShapeDtypeStruct((b, n, D_BRANCH), F32), jax.ShapeDtypeStruct((b, n, D_BRANCH), F32),
                   jax.ShapeDtypeStruct(s0p.shape, F32)],
        grid=(b, nblk),
        in_specs=[tok(fwd), tok(fwd), tok(fwd), tok2(fwd2), tok2(fwd2), tok2(fwd2),
                  tok(bwd), tok(bwd), tok(bwd), tok2(bwd2), tok2(bwd2), tok2(bwd2), _state_spec(s0p.shape)],
        out_specs=[tok(fwd), tok(bwd), _state_spec(s0p.shape)],
        scratch_shapes=[pltpu.VMEM(s0p.shape[1:], F32)],
        compiler_params=_params("parallel", "arbitrary"),
        name="rwkv_scan",
    )(r, v, kk, ld, kt, bb, r, v, kk, ld, kt, bb, s0p)
    sf = sfp.reshape(b, 2 * npair, HEAD_DIM, 2, HEAD_DIM).transpose(0, 1, 3, 2, 4).reshape(s0.shape)
    return yf, yb, sf


def _combine_kernel(x_ref, mod_ref, maf_ref, mab_ref, og_ref, lf_ref, lb_ref, lg_ref,
                    rf_ref, rb_ref, rg_ref, wf_ref, wb_ref, bonus_ref, wg_ref, ones_ref, wo_ref,
                    *rest, final):
    if final:
        fg_ref, o_ref = rest
    else:
        (o_ref,) = rest
    ones_bd = ones_ref[...]
    og = og_ref[0]
    h_a = (maf_ref[0] + mab_ref[0]) * _sigmoid(og[:, :D_BRANCH])
    y_a = _head_norm(h_a, ones_bd) * _silu(og[:, D_BRANCH:])
    y_b = (lf_ref[0] + lb_ref[0]) * _silu(lg_ref[0])
    y_c = _head_norm(rf_ref[0] + rb_ref[0], ones_bd) * _silu(rg_ref[0])
    y_d = (_head_norm(wf_ref[0] + wb_ref[0], ones_bd) + bonus_ref[0]) * _silu(wg_ref[0])
    y = (_dot(y_a, wo_ref[0:D_BRANCH, :]) + _dot(y_b, wo_ref[D_BRANCH:2 * D_BRANCH, :])
         + _dot(y_c, wo_ref[2 * D_BRANCH:3 * D_BRANCH, :]) + _dot(y_d, wo_ref[3 * D_BRANCH:, :]))
    x = x_ref[0] + mod_ref[0, 2:3, :] * y
    if final:
        x = x * lax.rsqrt(jnp.mean(x * x, axis=-1, keepdims=True) + EPS) * fg_ref[...]
    o_ref[0] = x


def _combine(x, mod, branch_arrays, ones_bd, w_out, final_g):
    b, n, _ = x.shape
    tm = TOKEN_BLOCK
    tok = lambda i, j: (i, j, 0)
    final = final_g is not None
    in_specs = [pl.BlockSpec((1, tm, D_MODEL), tok), pl.BlockSpec((1, 3, D_MODEL), lambda i, j: (i, 0, 0))]
    in_specs += [pl.BlockSpec((1, tm, a.shape[-1]), tok) for a in branch_arrays]
    in_specs += [_const_spec((D_BRANCH, D_BRANCH)), _const_spec(w_out.shape)]
    args = [x, mod, *branch_arrays, ones_bd, w_out]
    if final:
        in_specs.append(_const_spec((1, D_MODEL)))
        args.append(final_g)
    return pl.pallas_call(
        functools.partial(_combine_kernel, final=final),
        out_shape=jax.ShapeDtypeStruct(x.shape, F32),
        grid=(b, n // tm),
        in_specs=in_specs,
        out_specs=pl.BlockSpec((1, tm, D_MODEL), tok),
        compiler_params=_params("parallel", "parallel"),
        name="combine_outproj",
    )(*args)


def _prepare_weights(w_in, mlstm_gate_b, lru_gate_w):
    p_mlstm = 5 * D_BRANCH + 4 * N_HEADS
    a0, a1 = 0, p_mlstm
    b1 = a1 + 2 * D_BRANCH
    c1 = b1 + 4 * D_BRANCH
    wa = w_in[:, :, a0:a1]
    pad = jnp.zeros((DEPTH, D_MODEL, LANES - 4 * N_HEADS), w_in.dtype)
    w_a = jnp.concatenate([wa[:, :, :3 * D_BRANCH], wa[:, :, 5 * D_BRANCH:], pad,
                           wa[:, :, 3 * D_BRANCH:5 * D_BRANCH]], axis=-1).astype(BF16)
    w_b = w_in[:, :, a1:b1].astype(BF16)
    w_c = w_in[:, :, b1:c1].astype(BF16)
    w_d = w_in[:, :, c1:].astype(BF16)
    gate_b = jnp.pad(mlstm_gate_b, ((0, 0), (0, LANES - 4 * N_HEADS)))[:, None, :]
    eye_h = jnp.eye(N_HEADS, dtype=lru_gate_w.dtype)
    gw_bd = jnp.einsum('ldghij,hk->ldghikj', lru_gate_w, eye_h).reshape(
        DEPTH, 2, 2, D_BRANCH, D_BRANCH).astype(BF16)
    return w_a, w_b, w_c, w_d, gate_b, gw_bd


def _rope_tables(n):
    rows = n // GRID_W
    row_idx = jnp.broadcast_to(jnp.arange(rows, dtype=F32)[:, None], (rows, GRID_W)).reshape(-1)
    col_idx = jnp.broadcast_to(jnp.arange(GRID_W, dtype=F32)[None, :], (rows, GRID_W)).reshape(-1)
    n_freq = HEAD_DIM // 4
    freqs = ROPE_BASE ** (-jnp.arange(n_freq, dtype=F32) / n_freq)
    ang = jnp.concatenate([row_idx[:, None] * freqs, col_idx[:, None] * freqs], -1)
    cos, sin = jnp.cos(ang), jnp.sin(ang)
    cos_t = jnp.concatenate([cos, cos] * (LANES // HEAD_DIM), axis=-1)
    sin_t = jnp.concatenate([-sin, sin] * (LANES // HEAD_DIM), axis=-1)
    return cos_t, sin_t


def _layer(x, mod, lw, states, latent, rope_tabs, ones_bd, final_g):
    c0, n0, m0, h0, r0, s0 = states
    norm_g = lw['norm_g']
    u_a, og_a = _inproj(x, norm_g, mod, lw['w_a'], (3 * D_BRANCH + LANES, 2 * D_BRANCH))
    x_b, g_b = _inproj(x, norm_g, mod, lw['w_b'], (D_BRANCH, D_BRANCH))
    u_c, g_c = _inproj(x, norm_g, mod, lw['w_c'], (3 * D_BRANCH, D_BRANCH))
    s_d, g_d = _inproj(x, norm_g, mod, lw['w_d'], (RWKV_SHIFT, D_BRANCH))

    prep = _rwkv_prep(s_d, lw['rwkv_mu'], lw['rwkv_kk'], lw['rwkv_ka'], lw['rwkv_rk'], lw['rwkv_w0'],
                      lw['rwkv_w2'], lw['rwkv_a0'], lw['rwkv_a2'], ones_bd, latent)
    r_d, v_d, kk_d, bonus_d, ld_d, kt_d, bb_d = prep

    yaf, yab, c_new, n_new, m_new = _mlstm_scan(u_a, lw['gate_b'], c0, n0, m0)
    lru = [_lru_scan(x_b, lw['lru_conv_w'], lw['lru_conv_b'], lw['gw_bd'][d], lw['lru_gate_b'][d],
                     lw['lru_lambda'][d][None, :], h0[:, d][:, None, :], d) for d in range(2)]
    (ybf, hf), (ybb, hb) = lru
    ycf, ycb, r_new = _ret_scan(u_c, lw['theta'], r0, rope_tabs)
    ydf, ydb, s_new = _rwkv_scan(r_d, v_d, kk_d, ld_d, kt_d, bb_d, s0)
    branch_arrays = [yaf, yab, og_a, ybf, ybb, g_b, ycf, ycb, g_c, ydf, ydb, bonus_d, g_d]
    x_new = _combine(x, mod, branch_arrays, ones_bd, lw['w_out'], final_g)
    h_new = jnp.stack([hf[:, 0, :], hb[:, 0, :]], axis=1)
    return x_new, (c_new, n_new, m_new, h_new, r_new, s_new)


def kernel(x_prompt, x_sample, c, state_mlstm_c, state_mlstm_n, state_mlstm_m, state_lru_h, state_ret_r,
           state_rwkv_s, c_ctx, norm_g, w_mod, b_mod, w_in, w_out, mlstm_gate_b, lru_conv_w, lru_conv_b,
           lru_gate_w, lru_gate_b, lru_lambda, ret_theta, rwkv_mu, rwkv_w0, rwkv_w2, rwkv_a0, rwkv_a2,
           rwkv_kk, rwkv_ka, rwkv_rk, final_g):
    bp = x_prompt.shape[0]
    bs = x_sample.shape[0]
    assert x_prompt.shape[1] == TOKEN_BLOCK and x_sample.shape[1] % TOKEN_BLOCK == 0
    assert 1 + bs <= 8

    w_a, w_b, w_c, w_d, gate_b, gw_bd = _prepare_weights(w_in, mlstm_gate_b, lru_gate_w)
    w_out_bf = w_out.astype(BF16)
    head_of = jnp.arange(D_BRANCH) // HEAD_DIM
    ones_bd = (head_of[:, None] == head_of[None, :]).astype(BF16)
    theta = jnp.pad(ret_theta.reshape(DEPTH, 1, 2 * N_HEADS), ((0, 0), (0, 0), (0, LANES - 2 * N_HEADS)))
    rope_tabs = _rope_tables(x_sample.shape[1])

    cvec = jnp.concatenate([c_ctx[None, :], c, jnp.zeros((8 - 1 - bs, D_MODEL), F32)], axis=0)
    mods = _modulation(cvec, w_mod, b_mod).reshape(DEPTH, 8, 3, D_MODEL)

    def layer_weights(l):
        return dict(norm_g=norm_g[l][None, :], w_a=w_a[l], w_b=w_b[l], w_c=w_c[l], w_d=w_d[l],
                    gate_b=gate_b[l], gw_bd=gw_bd[l], lru_conv_w=lru_conv_w[l],
                    lru_conv_b=lru_conv_b[l][None, :], lru_gate_b=lru_gate_b[l], lru_lambda=lru_lambda[l],
                    theta=theta[l], rwkv_mu=rwkv_mu[l][None, :], rwkv_kk=rwkv_kk[l][None, :],
                    rwkv_ka=rwkv_ka[l][None, :], rwkv_rk=rwkv_rk[l][None, :], rwkv_w0=rwkv_w0[l],
                    rwkv_w2=rwkv_w2[l].astype(BF16), rwkv_a0=rwkv_a0[l], rwkv_a2=rwkv_a2[l].astype(BF16),
                    w_out=w_out_bf[l])

    zero_states = (jnp.zeros((bp, 2, N_HEADS, HEAD_DIM, HEAD_DIM), F32),
                   jnp.zeros((bp, 2, N_HEADS, HEAD_DIM), F32),
                   jnp.zeros((bp, 2, N_HEADS), F32),
                   jnp.zeros((bp, 2, D_BRANCH), F32),
                   jnp.zeros((bp, 2, N_HEADS, HEAD_DIM, HEAD_DIM), F32),
                   jnp.zeros((bp, 2, N_HEADS, HEAD_DIM, HEAD_DIM), F32))
    xp = x_prompt
    per_layer = []
    for l in range(DEPTH):
        mod = jnp.broadcast_to(mods[l, 0][None], (bp, 3, D_MODEL))
        xp, st = _layer(xp, mod, layer_weights(l), zero_states, False, None, ones_bd,
                        final_g[None, :] if l == DEPTH - 1 else None)
        per_layer.append(st)
    new_states = tuple(jnp.stack([st[j] for st in per_layer], axis=1) for j in range(6))

    xs = x_sample
    for l in range(DEPTH):
        mod = mods[l, 1:1 + bs]
        st = (state_mlstm_c[:, l], state_mlstm_n[:, l], state_mlstm_m[:, l], state_lru_h[:, l],
              state_ret_r[:, l], state_rwkv_s[:, l])
        xs, _ = _layer(xs, mod, layer_weights(l), st, True, rope_tabs, ones_bd,
                       final_g[None, :] if l == DEPTH - 1 else None)
    return (xp, xs) + new_states
```

```python
import functools

import jax
import jax.numpy as jnp
from jax import lax
from jax.experimental import pallas as pl
from jax.experimental.pallas import tpu as pltpu

F32 = jnp.float32
BF16 = jnp.bfloat16
HIGHEST = lax.Precision.HIGHEST

D_MODEL = 1024
DEPTH = 4
GRID_W = 64
D_BRANCH = 512
HEAD_DIM = 64
N_HEADS = 8
CHUNK = 64
CONV_W = 4
LRU_C = 8.0
LORA = 64
ROPE_BASE = 100.0
EPS = 1e-6
RWKV_SHIFT = 3 * D_BRANCH + 2 * LORA

LANES = 128
TOKEN_BLOCK = 256
VMEM_LIMIT = 48 * 1024 * 1024
NEG = -1e30


def _sigmoid(x):
    return 1.0 / (1.0 + jnp.exp(-x))


def _silu(x):
    return x * _sigmoid(x)


def _softplus(x):
    return jnp.maximum(x, 0.0) + jnp.log(1.0 + jnp.exp(-jnp.abs(x)))


def _log_sigmoid(x):
    return -_softplus(-x)


def _dot(a, b):
    return jnp.dot(a.astype(BF16), b.astype(BF16), preferred_element_type=F32)


def _dot_nt(a, b):
    return lax.dot_general(a.astype(BF16), b.astype(BF16), (((1,), (1,)), ((), ())),
                           preferred_element_type=F32)


def _dot_tn(a, b):
    return lax.dot_general(a.astype(BF16), b.astype(BF16), (((0,), (0,)), ((), ())),
                           preferred_element_type=F32)


def _dot_hi(a, b):
    return jnp.dot(a, b, preferred_element_type=F32, precision=HIGHEST)


def _dot_nt_hi(a, b):
    return lax.dot_general(a, b, (((1,), (1,)), ((), ())), preferred_element_type=F32,
                           precision=HIGHEST)


_NN = (((1,), (0,)), ((), ()))
_NT = (((1,), (1,)), ((), ()))
_TN = (((0,), (0,)), ((), ()))
_BNN = (((2,), (1,)), ((0,), (0,)))
_BNT = (((2,), (2,)), ((0,), (0,)))
_RWKV_PREC = dict(a=1, inv=3, merge=1, solve=3, rest=1)


def _split_heads(x):
    return jnp.stack([x[:, h * HEAD_DIM:(h + 1) * HEAD_DIM] for h in range(N_HEADS)], axis=0)


def _mm(a, b, dims, passes):
    if passes == 6:
        return lax.dot_general(a, b, dims, preferred_element_type=F32, precision=HIGHEST)
    ah = a.astype(BF16)
    bh = b.astype(BF16)
    out = lax.dot_general(ah, bh, dims, preferred_element_type=F32)
    if passes == 3:
        al = (a - ah.astype(F32)).astype(BF16)
        bl = (b - bh.astype(F32)).astype(BF16)
        out = (out + lax.dot_general(ah, bl, dims, preferred_element_type=F32)
               + lax.dot_general(al, bh, dims, preferred_element_type=F32))
    return out


def _mm_shared(lhs, b, passes):
    n, m = len(lhs), lhs[0].shape[1]
    bh = b.astype(BF16)
    hs = [x.astype(BF16) for x in lhs]
    if passes == 1:
        out = lax.dot_general(jnp.concatenate(hs, axis=1), bh, _BNN, preferred_element_type=F32)
        return [out[:, i * m:(i + 1) * m] for i in range(n)]
    assert passes == 3
    ls = [(x - h.astype(F32)).astype(BF16) for x, h in zip(lhs, hs)]
    bl = (b - bh.astype(F32)).astype(BF16)
    o1 = lax.dot_general(jnp.concatenate(hs + ls, axis=1), bh, _BNN, preferred_element_type=F32)
    o2 = lax.dot_general(jnp.concatenate(hs, axis=1), bl, _BNN, preferred_element_type=F32)
    return [o1[:, i * m:(i + 1) * m] + o1[:, (n + i) * m:(n + i + 1) * m] + o2[:, i * m:(i + 1) * m]
            for i in range(n)]


def _head_sum(x, ones_bd):
    hi = x.astype(BF16)
    lo = (x - hi.astype(F32)).astype(BF16)
    return (jnp.dot(hi, ones_bd, preferred_element_type=F32)
            + jnp.dot(lo, ones_bd, preferred_element_type=F32))


def _head_norm(x, ones_bd):
    return x * lax.rsqrt(_head_sum(x * x, ones_bd) * (1.0 / HEAD_DIM) + EPS)


def _tri_masks(n, reverse):
    row = lax.broadcasted_iota(jnp.int32, (n, n), 0)
    col = lax.broadcasted_iota(jnp.int32, (n, n), 1)
    incl = (col >= row) if reverse else (col <= row)
    strict = (col > row) if reverse else (col < row)
    return incl, strict, row, col


def _eye(n):
    row = lax.broadcasted_iota(jnp.int32, (n, n), 0)
    col = lax.broadcasted_iota(jnp.int32, (n, n), 1)
    return jnp.where(row == col, 1.0, 0.0).astype(F32)


def _params(*sem):
    return pltpu.CompilerParams(dimension_semantics=sem, vmem_limit_bytes=VMEM_LIMIT)


def _blk_index(reverse, nblk):
    if reverse:
        return lambda b, i: (b, nblk - 1 - i, 0)
    return lambda b, i: (b, i, 0)


def _state_spec(shape):
    nd = len(shape)
    return pl.BlockSpec((1,) + tuple(shape[1:]), lambda b, i: (b,) + (0,) * (nd - 1))


def _const_spec(shape):
    nd = len(shape)
    return pl.BlockSpec(tuple(shape), lambda *_: (0,) * nd)


def _mod_kernel(c_ref, w_ref, b_ref, o_ref):
    sc = _silu(c_ref[...])
    o_ref[0] = _dot(sc, w_ref[0]) + b_ref[0]


def _modulation(cvec, w_mod, b_mod):
    nt = 3
    return pl.pallas_call(
        _mod_kernel,
        out_shape=jax.ShapeDtypeStruct((DEPTH, 8, 3 * D_MODEL), F32),
        grid=(DEPTH, nt),
        in_specs=[pl.BlockSpec((8, D_MODEL), lambda l, j: (0, 0)),
                  pl.BlockSpec((1, D_MODEL, D_MODEL), lambda l, j: (l, 0, j)),
                  pl.BlockSpec((1, 1, D_MODEL), lambda l, j: (l, 0, j))],
        out_specs=pl.BlockSpec((1, 8, D_MODEL), lambda l, j: (l, 0, j)),
        compiler_params=_params("arbitrary", "arbitrary"),
        name="modulation",
    )(cvec, w_mod, b_mod.reshape(DEPTH, 1, 3 * D_MODEL))


def _inproj_kernel(x_ref, g_ref, mod_ref, w_ref, *o_refs, widths):
    x = x_ref[0]
    y = x * lax.rsqrt(jnp.mean(x * x, axis=-1, keepdims=True) + EPS) * g_ref[...]
    h = y * (1.0 + mod_ref[0, 1:2, :]) + mod_ref[0, 0:1, :]
    u = _dot(h, w_ref[...])
    off = 0
    for o_ref, w in zip(o_refs, widths):
        o_ref[0] = u[:, off:off + w]
        off += w


def _inproj(x, g, mod, w, widths):
    b, n, _ = x.shape
    p = w.shape[1]
    tm = TOKEN_BLOCK
    return pl.pallas_call(
        functools.partial(_inproj_kernel, widths=widths),
        out_shape=[jax.ShapeDtypeStruct((b, n, wd), F32) for wd in widths],
        grid=(b, n // tm),
        in_specs=[pl.BlockSpec((1, tm, D_MODEL), lambda i, j: (i, j, 0)),
                  _const_spec((1, D_MODEL)),
                  pl.BlockSpec((1, 3, D_MODEL), lambda i, j: (i, 0, 0)),
                  _const_spec((D_MODEL, p))],
        out_specs=[pl.BlockSpec((1, tm, wd), lambda i, j: (i, j, 0)) for wd in widths],
        compiler_params=_params("parallel", "parallel"),
        name="inproj",
    )(x, g, mod, w)


def _expand_exact(x, sel):
    m = x.shape[0]
    hi = x.astype(BF16)
    r1 = x - hi.astype(F32)
    mid = r1.astype(BF16)
    lo = (r1 - mid.astype(F32)).astype(BF16)
    out = jnp.dot(jnp.concatenate([hi, mid, lo], axis=0), sel, preferred_element_type=F32)
    return out[:m] + out[m:2 * m] + out[2 * m:]


def _mlstm_kernel(uf_ref, ub_ref, gb_ref, c0_ref, n0_ref, m0_ref, yf_ref, yb_ref, cf_ref, nf_ref, mf_ref,
                  c_s, n_s, m_s):
    @pl.when(pl.program_id(1) == 0)
    def _():
        _load_blockdiag(c0_ref, c_s)
        n_s[...] = n0_ref[0]
        m_s[...] = m0_ref[0]

    L = CHUNK
    nchunk = TOKEN_BLOCK // L
    npair = D_BRANCH // LANES
    lane = lax.broadcasted_iota(jnp.int32, (L, LANES), 1)
    left = lane < HEAD_DIM
    row_w = lax.broadcasted_iota(jnp.int32, (L, D_BRANCH), 0)
    col_w = lax.broadcasted_iota(jnp.int32, (L, D_BRANCH), 1) & (HEAD_DIM - 1)
    incl_w = (col_w <= row_w, col_w >= row_w)
    diag_w = col_w == row_w
    row_g = lax.broadcasted_iota(jnp.int32, (L, LANES), 0)
    row_l = lax.broadcasted_iota(jnp.int32, (L, L), 0)
    col_l = lax.broadcasted_iota(jnp.int32, (L, L), 1)
    tri = (jnp.where(col_l <= row_l, 1.0, 0.0).astype(F32), jnp.where(col_l >= row_l, 1.0, 0.0).astype(F32))
    sel_r = lax.broadcasted_iota(jnp.int32, (LANES, D_BRANCH), 0)
    sel_h = lax.broadcasted_iota(jnp.int32, (LANES, D_BRANCH), 1) >> 6
    sel_ig = [jnp.where(sel_r == 2 * N_HEADS * d + sel_h, 1.0, 0.0).astype(BF16) for d in range(2)]
    sel_fg = [jnp.where(sel_r == 2 * N_HEADS * d + N_HEADS + sel_h, 1.0, 0.0).astype(BF16) for d in range(2)]
    sq_r = lax.broadcasted_iota(jnp.int32, (LANES, LANES), 0) >> 6
    sq_c = lax.broadcasted_iota(jnp.int32, (LANES, LANES), 1) >> 6
    same_head = sq_r == sq_c
    ones_bd = jnp.where(same_head, 1.0, 0.0).astype(BF16)
    u_refs = (uf_ref, ub_ref)
    scale = HEAD_DIM ** -0.5

    def chunk(ci, carry):
        rows = (pl.ds(pl.multiple_of(ci * L, L), L), pl.ds(pl.multiple_of((nchunk - 1 - ci) * L, L), L))
        parts = []
        for d in range(2):
            u_ref = u_refs[d]
            last = 0 if d == 1 else L - 1
            gates = u_ref[0, rows[d], 3 * D_BRANCH:3 * D_BRANCH + LANES] + gb_ref[...]
            bmat = _dot_hi(tri[d], _log_sigmoid(gates))
            b_at_ig = pltpu.roll(bmat, LANES - N_HEADS, axis=1)
            cmx = gates - b_at_ig
            sh = 1
            while sh < L:
                if d == 0:
                    cmx = jnp.where(row_g >= sh, jnp.maximum(cmx, pltpu.roll(cmx, sh, axis=0)), cmx)
                else:
                    cmx = jnp.where(row_g < L - sh, jnp.maximum(cmx, pltpu.roll(cmx, L - sh, axis=0)), cmx)
                sh *= 2
            rmax = _expand_exact(b_at_ig + cmx, sel_ig[d])
            bcol = _expand_exact(bmat, sel_fg[d])
            igcol = _expand_exact(gates, sel_ig[d])
            crow = jnp.sum(jnp.where(diag_w, igcol - bcol, 0.0), axis=0, keepdims=True)
            m_prev = m_s[d]
            m_inter = bcol + m_prev
            m_t = jnp.maximum(m_inter, rmax)
            pexp = jnp.exp(jnp.where(incl_w[d], bcol + crow, NEG) - m_t)
            w_inter = jnp.exp(m_inter - m_t)
            em = jnp.exp(-m_t)
            b_last = bcol[last:last + 1, :]
            w_log = b_last - bcol + igcol
            m_new = jnp.maximum(b_last + m_prev, jnp.max(w_log, axis=0, keepdims=True))
            dec = jnp.exp(b_last + m_prev - m_new)
            wdec = jnp.exp(w_log - m_new)
            m_s[d] = m_new
            q = u_ref[0, rows[d], 0:D_BRANCH]
            k = u_ref[0, rows[d], D_BRANCH:2 * D_BRANCH] * scale
            v = u_ref[0, rows[d], 2 * D_BRANCH:3 * D_BRANCH]
            parts.append([_pair_split(t) for t in (q, k, v, pexp, w_inter, em, k * wdec, dec)])
        qp, kp, vp, pexp, w_inter, em, wk, dec = (jnp.concatenate(t, axis=0) for t in zip(*parts))
        c_prev = c_s[...]
        n_prev = n_s[...]
        s = _mm(qp, _pair_blockdiag(kp, left), _BNT, 1) * pexp
        num = _mm(s, _pair_blockdiag(vp, left), _BNN, 1) + w_inter * _mm(qp, c_prev, _BNN, 1)
        both = jnp.concatenate([s, qp * n_prev], axis=1).reshape(2 * npair * 2 * L, LANES)
        sums = _head_sum(both, ones_bd).reshape(2 * npair, 2 * L, LANES)
        den = sums[:, :L] + w_inter * sums[:, L:]
        hh = num / jnp.maximum(jnp.abs(den), em)
        n_s[...] = dec * n_prev + jnp.sum(wk, axis=1, keepdims=True)
        for g in range(2 * npair):
            c_s[g] = dec[g] * c_prev[g] + jnp.where(same_head, _dot_tn(wk[g], vp[g]), 0.0)
        for p in range(npair):
            yf_ref[0, rows[0], p * LANES:(p + 1) * LANES] = hh[p]
            yb_ref[0, rows[1], p * LANES:(p + 1) * LANES] = hh[npair + p]
        return carry

    lax.fori_loop(0, nchunk, chunk, 0)

    @pl.when(pl.program_id(1) == pl.num_programs(1) - 1)
    def _():
        _store_blockdiag(c_s, cf_ref)
        nf_ref[0] = n_s[...]
        mf_ref[0] = m_s[...]


def _mlstm_scan(u, gate_b, c0, n0, m0):
    b, n, p = u.shape
    nblk = n // TOKEN_BLOCK
    groups = 2 * (D_BRANCH // LANES)
    fwd = lambda bb, i: (bb, i, 0)
    bwd = lambda bb, i: (bb, nblk - 1 - i, 0)
    c0p = _heads_flat(c0)
    n0p = n0.reshape(b, groups, 1, LANES)
    m0p = jnp.repeat(m0, HEAD_DIM, axis=-1)[:, :, None, :]
    states = (c0p, n0p, m0p)
    yf, yb, cfp, nfp, mfp = pl.pallas_call(
        _mlstm_kernel,
        out_shape=[jax.ShapeDtypeStruct((b, n, D_BRANCH), F32), jax.ShapeDtypeStruct((b, n, D_BRANCH), F32)]
                  + [jax.ShapeDtypeStruct(s.shape, F32) for s in states],
        grid=(b, nblk),
        in_specs=[pl.BlockSpec((1, TOKEN_BLOCK, p), fwd), pl.BlockSpec((1, TOKEN_BLOCK, p), bwd),
                  _const_spec((1, LANES))] + [_state_spec(s.shape) for s in states],
        out_specs=[pl.BlockSpec((1, TOKEN_BLOCK, D_BRANCH), fwd), pl.BlockSpec((1, TOKEN_BLOCK, D_BRANCH), bwd)]
                  + [_state_spec(s.shape) for s in states],
        scratch_shapes=[pltpu.VMEM((groups, LANES, LANES), F32), pltpu.VMEM(n0p.shape[1:], F32),
                        pltpu.VMEM(m0p.shape[1:], F32)],
        compiler_params=_params("parallel", "arbitrary"),
        name="mlstm_scan",
    )(u, u, gate_b, *states)
    return yf, yb, cfp.reshape(c0.shape), nfp.reshape(n0.shape), mfp[:, :, 0, ::HEAD_DIM]


def _ret_kernel(*refs, rope):
    if rope:
        uf_ref, ub_ref, th_ref, cosf_ref, sinf_ref, cosb_ref, sinb_ref, r0_ref, yf_ref, yb_ref, rf_ref, r_s = refs
        tabs = ((cosf_ref, sinf_ref), (cosb_ref, sinb_ref))
    else:
        uf_ref, ub_ref, th_ref, r0_ref, yf_ref, yb_ref, rf_ref, r_s = refs
    u_refs = (uf_ref, ub_ref)

    @pl.when(pl.program_id(1) == 0)
    def _():
        _load_blockdiag(r0_ref, r_s)

    L = CHUNK
    nchunk = TOKEN_BLOCK // L
    npair = D_BRANCH // LANES
    row = lax.broadcasted_iota(jnp.int32, (L, LANES), 0)
    lane = lax.broadcasted_iota(jnp.int32, (L, LANES), 1)
    col = lane & (HEAD_DIM - 1)
    left = lane < HEAD_DIM
    left_sq = lax.broadcasted_iota(jnp.int32, (LANES, LANES), 1) < HEAD_DIM
    same_head = (lax.broadcasted_iota(jnp.int32, (LANES, LANES), 0) < HEAD_DIM) == left_sq

    def both(fwd, bwd):
        return jnp.concatenate([jnp.broadcast_to(fwd, (npair, L, LANES)),
                                jnp.broadcast_to(bwd, (npair, L, LANES))], axis=0)

    incl = both(col <= row, col >= row)
    diff = jnp.abs(row - col).astype(F32)
    rowf = row.astype(F32)
    p_pos = both(rowf, L - 1.0 - rowf)
    log_g = _log_sigmoid(th_ref[...])
    lg = jnp.stack([jnp.where(left[0:1], log_g[:, N_HEADS * d + 2 * p:N_HEADS * d + 2 * p + 1],
                              log_g[:, N_HEADS * d + 2 * p + 1:N_HEADS * d + 2 * p + 2])
                    for d in range(2) for p in range(npair)], axis=0)
    dmat = jnp.where(incl, jnp.exp(lg * diff), 0.0)
    xi = jnp.exp(lg * (p_pos + 1.0))
    wk = jnp.exp(lg * (L - 1.0 - p_pos))
    dec = jnp.exp(lg * float(L))
    lane_w = lax.broadcasted_iota(jnp.int32, (L, D_BRANCH), 1)
    first_half = (lane_w & (HEAD_DIM - 1)) < (HEAD_DIM // 2)

    def rot(x, c, s):
        swapped = jnp.where(first_half, pltpu.roll(x, D_BRANCH - HEAD_DIM // 2, axis=1),
                            pltpu.roll(x, HEAD_DIM // 2, axis=1))
        return x * c + swapped * s

    def chunk(ci, carry):
        rows = (pl.ds(pl.multiple_of(ci * L, L), L), pl.ds(pl.multiple_of((nchunk - 1 - ci) * L, L), L))
        qs, ks, vs = [], [], []
        for d in range(2):
            q = u_refs[d][0, rows[d], 0:D_BRANCH]
            k = u_refs[d][0, rows[d], D_BRANCH:2 * D_BRANCH]
            if rope:
                c = jnp.concatenate([tabs[d][0][rows[d], :]] * npair, axis=1)
                s = jnp.concatenate([tabs[d][1][rows[d], :]] * npair, axis=1)
                q = rot(q, c, s)
                k = rot(k, c, s)
            qs.append(_pair_split(q))
            ks.append(_pair_split(k * (HEAD_DIM ** -0.5)))
            vs.append(_pair_split(u_refs[d][0, rows[d], 2 * D_BRANCH:3 * D_BRANCH]))
        qp, kp, vp = (jnp.concatenate(t, axis=0) for t in (qs, ks, vs))
        r_prev = r_s[...]
        s_mat = _mm(qp, _pair_blockdiag(kp, left), _BNT, 1) * dmat
        y = _mm(s_mat, _pair_blockdiag(vp, left), _BNN, 1) + xi * _mm(qp, r_prev, _BNN, 1)
        kw = kp * wk
        for g in range(2 * npair):
            r_s[g] = dec[g] * r_prev[g] + jnp.where(same_head, _dot_tn(kw[g], vp[g]), 0.0)
        for p in range(npair):
            yf_ref[0, rows[0], p * LANES:(p + 1) * LANES] = y[p]
            yb_ref[0, rows[1], p * LANES:(p + 1) * LANES] = y[npair + p]
        return carry

    lax.fori_loop(0, nchunk, chunk, 0)

    @pl.when(pl.program_id(1) == pl.num_programs(1) - 1)
    def _():
        _store_blockdiag(r_s, rf_ref)


def _load_blockdiag(src_ref, dst_s):
    zero = jnp.zeros((HEAD_DIM, HEAD_DIM), F32)
    for g in range(dst_s.shape[0]):
        top = jnp.concatenate([src_ref[0, 2 * g], zero], axis=1)
        bot = jnp.concatenate([zero, src_ref[0, 2 * g + 1]], axis=1)
        dst_s[g] = jnp.concatenate([top, bot], axis=0)


def _store_blockdiag(src_s, dst_ref):
    for g in range(src_s.shape[0]):
        blk = src_s[g]
        dst_ref[0, 2 * g] = blk[:HEAD_DIM, :HEAD_DIM]
        dst_ref[0, 2 * g + 1] = blk[HEAD_DIM:, HEAD_DIM:]


def _heads_flat(s):
    return s.reshape(s.shape[0], 2 * N_HEADS, HEAD_DIM, HEAD_DIM)


def _ret_scan(u, theta, r0, rope_tabs):
    b, n, p = u.shape
    nblk = n // TOKEN_BLOCK
    fwd = lambda bb, i: (bb, i, 0)
    bwd = lambda bb, i: (bb, nblk - 1 - i, 0)
    rope = rope_tabs is not None
    r0p = _heads_flat(r0)
    in_specs = [pl.BlockSpec((1, TOKEN_BLOCK, p), fwd), pl.BlockSpec((1, TOKEN_BLOCK, p), bwd),
                _const_spec((1, LANES))]
    args = [u, u, theta]
    if rope:
        in_specs += [pl.BlockSpec((TOKEN_BLOCK, LANES), lambda bb, i: (i, 0))] * 2
        in_specs += [pl.BlockSpec((TOKEN_BLOCK, LANES), lambda bb, i: (nblk - 1 - i, 0))] * 2
        args += list(rope_tabs) * 2
    in_specs.append(_state_spec(r0p.shape))
    args.append(r0p)
    yf, yb, rfp = pl.pallas_call(
        functools.partial(_ret_kernel, rope=rope),
        out_shape=[jax.ShapeDtypeStruct((b, n, D_BRANCH), F32), jax.ShapeDtypeStruct((b, n, D_BRANCH), F32),
                   jax.ShapeDtypeStruct(r0p.shape, F32)],
        grid=(b, nblk),
        in_specs=in_specs,
        out_specs=[pl.BlockSpec((1, TOKEN_BLOCK, D_BRANCH), fwd), pl.BlockSpec((1, TOKEN_BLOCK, D_BRANCH), bwd),
                   _state_spec(r0p.shape)],
        scratch_shapes=[pltpu.VMEM((2 * (D_BRANCH // LANES), LANES, LANES), F32)],
        compiler_params=_params("parallel", "arbitrary"),
        name="retention_scan",
    )(*args)
    return yf, yb, rfp.reshape(r0.shape)


def _lru_kernel(x_ref, xp_ref, xn_ref, cw_ref, cb_ref, gw_ref, gb_ref, lam_ref, h0_ref,
                y_ref, hf_ref, h_s, *, reverse):
    i = pl.program_id(1)
    nblk = pl.num_programs(1)
    blk = (nblk - 1 - i) if reverse else i

    @pl.when(i == 0)
    def _():
        h_s[...] = h0_ref[0]

    tb = TOKEN_BLOCK
    x = x_ref[0]
    row = lax.broadcasted_iota(jnp.int32, (tb, D_BRANCH), 0)
    prev_ok = jnp.where(blk > 0, 1.0, 0.0)
    next_ok = jnp.where(blk < nblk - 1, 1.0, 0.0)
    p_last = xp_ref[0, 7:8, :] * prev_ok
    n_0 = xn_ref[0, 0:1, :] * next_ok
    n_1 = xn_ref[0, 1:2, :] * next_ok
    xm1 = jnp.where(row == 0, p_last, pltpu.roll(x, 1, axis=0))
    xp1 = jnp.where(row == tb - 1, n_0, pltpu.roll(x, tb - 1, axis=0))
    xp2 = jnp.where(row == tb - 1, n_1, jnp.where(row == tb - 2, n_0, pltpu.roll(x, tb - 2, axis=0)))
    xc = (cw_ref[0:1, :] * xm1 + cw_ref[1:2, :] * x + cw_ref[2:3, :] * xp1 + cw_ref[3:4, :] * xp2
          + cb_ref[...])
    gr = _dot(xc, gw_ref[0]) + gb_ref[0:1, :]
    gi = _dot(xc, gw_ref[1]) + gb_ref[1:2, :]
    log_a = -LRU_C * _sigmoid(gr) * _softplus(-lam_ref[...])
    a = jnp.exp(log_a)
    beta = jnp.sqrt(1.0 - a * a)
    bx = beta * _sigmoid(gi) * xc

    sh = 1
    while sh < tb:
        if sh < 8:
            if reverse:
                ok = row < tb - sh
                a_sh = pltpu.roll(a, tb - sh, axis=0)
                b_sh = pltpu.roll(bx, tb - sh, axis=0)
            else:
                ok = row >= sh
                a_sh = pltpu.roll(a, sh, axis=0)
                b_sh = pltpu.roll(bx, sh, axis=0)
            bx = jnp.where(ok, a * b_sh + bx, bx)
            a = jnp.where(ok, a * a_sh, a)
        elif reverse:
            bx = jnp.concatenate([a[:tb - sh] * bx[sh:] + bx[:tb - sh], bx[tb - sh:]], axis=0)
            a = jnp.concatenate([a[:tb - sh] * a[sh:], a[tb - sh:]], axis=0)
        else:
            bx = jnp.concatenate([bx[:sh], a[sh:] * bx[:tb - sh] + bx[sh:]], axis=0)
            a = jnp.concatenate([a[:sh], a[sh:] * a[:tb - sh]], axis=0)
        sh *= 2
    h = a * h_s[...] + bx
    y_ref[0] = h
    h_s[...] = h[0:1, :] if reverse else h[tb - 1:tb, :]

    @pl.when(i == nblk - 1)
    def _():
        hf_ref[0] = h_s[...]


def _lru_scan(xb, conv_w, conv_b, gate_w_bd, gate_b, lam, h0, direction):
    b, n, _ = xb.shape
    tb = TOKEN_BLOCK
    nblk = n // tb
    reverse = direction == 1
    idx = _blk_index(reverse, nblk)
    sub = tb // 8
    nsub = n // 8
    if reverse:
        prev_idx = lambda bb, i: (bb, jnp.maximum((nblk - 1 - i) * sub - 1, 0), 0)
        next_idx = lambda bb, i: (bb, jnp.minimum((nblk - i) * sub, nsub - 1), 0)
    else:
        prev_idx = lambda bb, i: (bb, jnp.maximum(i * sub - 1, 0), 0)
        next_idx = lambda bb, i: (bb, jnp.minimum((i + 1) * sub, nsub - 1), 0)
    return pl.pallas_call(
        functools.partial(_lru_kernel, reverse=reverse),
        out_shape=[jax.ShapeDtypeStruct((b, n, D_BRANCH), F32), jax.ShapeDtypeStruct(h0.shape, F32)],
        grid=(b, nblk),
        in_specs=[pl.BlockSpec((1, tb, D_BRANCH), idx),
                  pl.BlockSpec((1, 8, D_BRANCH), prev_idx),
                  pl.BlockSpec((1, 8, D_BRANCH), next_idx),
                  _const_spec((CONV_W, D_BRANCH)), _const_spec((1, D_BRANCH)),
                  _const_spec((2, D_BRANCH, D_BRANCH)), _const_spec((2, D_BRANCH)),
                  _const_spec((1, D_BRANCH)), _state_spec(h0.shape)],
        out_specs=[pl.BlockSpec((1, tb, D_BRANCH), idx), _state_spec(h0.shape)],
        scratch_shapes=[pltpu.VMEM((1, D_BRANCH), F32)],
        compiler_params=_params("parallel", "arbitrary"),
        name="rglru_scan",
    )(xb, xb, xb, conv_w, conv_b, gate_w_bd, gate_b, lam, h0)


def _rwkv_prep_kernel(s_ref, sp_ref, sn_ref, mu_ref, kk_ref, ka_ref, rk_ref, w0_ref, w2_ref,
                      a0_ref, a2_ref, ones_ref,
                      r_ref, v_ref, kkn_ref, bonus_ref, ld_ref, kt_ref, bb_ref, *, grid_shift):
    tb = TOKEN_BLOCK
    s = s_ref[0]
    width = s.shape[1]
    row = lax.broadcasted_iota(jnp.int32, (tb, width), 0)
    lane = lax.broadcasted_iota(jnp.int32, (tb, width), 1)
    if grid_shift:
        i = pl.program_id(1)
        nblk = pl.num_programs(1)
        col = row & (GRID_W - 1)
        qc = width // 4
        up_halo = sp_ref[0] * jnp.where(i > 0, 1.0, 0.0)
        dn_halo = sn_ref[0] * jnp.where(i < nblk - 1, 1.0, 0.0)
        left = jnp.where(col == 0, 0.0, pltpu.roll(s, 1, axis=0))
        right = jnp.where(col == GRID_W - 1, 0.0, pltpu.roll(s, tb - 1, axis=0))
        up = jnp.concatenate([up_halo, s[:tb - GRID_W]], axis=0)
        down = jnp.concatenate([s[GRID_W:], dn_halo], axis=0)
        sh = jnp.where(lane < qc, left,
                       jnp.where(lane < 2 * qc, right, jnp.where(lane < 3 * qc, up, down)))
    else:
        prev = jnp.where(row == 0, 0.0, pltpu.roll(s, 1, axis=0))
        nxt = jnp.where(row == tb - 1, 0.0, pltpu.roll(s, tb - 1, axis=0))
        sh = jnp.where(lane < width // 2, prev, nxt)
    s = s + mu_ref[...] * (sh - s)
    r = s[:, 0:D_BRANCH]
    k = s[:, D_BRANCH:2 * D_BRANCH]
    v = s[:, 2 * D_BRANCH:3 * D_BRANCH]
    lw = jnp.tanh(s[:, 3 * D_BRANCH:3 * D_BRANCH + LORA])
    la = s[:, 3 * D_BRANCH + LORA:3 * D_BRANCH + 2 * LORA]
    ones_bd = ones_ref[...]
    kkh = k * kk_ref[...]
    kk = kkh / jnp.maximum(jnp.sqrt(_head_sum(kkh * kkh, ones_bd)), 1e-12)
    r_ref[0] = r
    v_ref[0] = v
    kkn_ref[0] = kk
    bonus_ref[0] = _head_sum(r * k * rk_ref[...], ones_bd) * v
    for d in range(2):
        w_log = -_softplus(-(w0_ref[d:d + 1, :] + _dot(lw, w2_ref[d]))) - 0.5
        ld_ref[d, 0] = -jnp.exp(w_log)
        a = _sigmoid(a0_ref[d:d + 1, :] + _dot(la, a2_ref[d]))
        kt_ref[d, 0] = k * (1.0 + (a - 1.0) * ka_ref[...])
        bb_ref[d, 0] = kk * a


def _rwkv_prep(s, mu, k_k, k_a, r_k, w0, w2, a0, a2, ones_bd, grid_shift):
    b, n, width = s.shape
    tb = TOKEN_BLOCK
    nblk = n // tb
    assert grid_shift or nblk == 1
    halo = GRID_W
    sub = tb // halo
    nsub = n // halo
    tok = lambda bb, i: (bb, i, 0)
    tok2 = lambda bb, i: (0, bb, i, 0)
    out1 = jax.ShapeDtypeStruct((b, n, D_BRANCH), F32)
    out2 = jax.ShapeDtypeStruct((2, b, n, D_BRANCH), F32)
    return pl.pallas_call(
        functools.partial(_rwkv_prep_kernel, grid_shift=grid_shift),
        out_shape=[out1, out1, out1, out1, out2, out2, out2],
        grid=(b, nblk),
        in_specs=[pl.BlockSpec((1, tb, width), tok),
                  pl.BlockSpec((1, halo, width), lambda bb, i: (bb, jnp.maximum(i * sub - 1, 0), 0)),
                  pl.BlockSpec((1, halo, width), lambda bb, i: (bb, jnp.minimum((i + 1) * sub, nsub - 1), 0)),
                  _const_spec((1, width)), _const_spec((1, D_BRANCH)), _const_spec((1, D_BRANCH)),
                  _const_spec((1, D_BRANCH)), _const_spec((2, D_BRANCH)),
                  _const_spec((2, LORA, D_BRANCH)), _const_spec((2, D_BRANCH)),
                  _const_spec((2, LORA, D_BRANCH)), _const_spec((D_BRANCH, D_BRANCH))],
        out_specs=[pl.BlockSpec((1, tb, D_BRANCH), tok)] * 4
                  + [pl.BlockSpec((2, 1, tb, D_BRANCH), tok2)] * 3,
        compiler_params=_params("parallel", "parallel"),
        name="rwkv_prep",
    )(s, s, s, mu, k_k, k_a, r_k, w0, w2, a0, a2, ones_bd)


def _pair_split(x):
    return jnp.stack([x[:, p * LANES:(p + 1) * LANES] for p in range(D_BRANCH // LANES)], axis=0)


def _pair_blockdiag(p, left):
    return jnp.concatenate([jnp.where(left, p, 0.0), jnp.where(left, 0.0, p)], axis=1)


def _rwkv_kernel(rf_ref, vf_ref, kkf_ref, ldf_ref, ktf_ref, bbf_ref,
                 rb_ref, vb_ref, kkb_ref, ldb_ref, ktb_ref, bbb_ref, s0_ref,
                 yf_ref, yb_ref, sf_ref, s_s):
    @pl.when(pl.program_id(1) == 0)
    def _():
        for g in range(s_s.shape[0]):
            s_s[g] = jnp.concatenate([s0_ref[0, 2 * g], s0_ref[0, 2 * g + 1]], axis=1)

    L = CHUNK
    nchunk = TOKEN_BLOCK // L
    npair = D_BRANCH // LANES
    row = lax.broadcasted_iota(jnp.int32, (L, LANES), 0)
    lane = lax.broadcasted_iota(jnp.int32, (L, LANES), 1)
    col = lane & (HEAD_DIM - 1)
    left = lane < HEAD_DIM

    def both(fwd, bwd):
        return jnp.concatenate([jnp.broadcast_to(fwd, (npair, L, LANES)),
                                jnp.broadcast_to(bwd, (npair, L, LANES))], axis=0)

    incl = both(col <= row, col >= row)
    strict = both(col < row, col > row)
    eye_p = jnp.where(col == row, 1.0, 0.0).astype(F32)
    same16 = (row >> 4) == (col >> 4)
    same32 = (row >> 5) == (col >> 5)
    off32 = jnp.logical_and(same32, jnp.logical_not(same16))
    off64 = jnp.logical_not(same32)
    row_l = lax.broadcasted_iota(jnp.int32, (L, L), 0)
    col_l = lax.broadcasted_iota(jnp.int32, (L, L), 1)
    tri = (jnp.where(col_l <= row_l, 1.0, 0.0).astype(F32), jnp.where(col_l >= row_l, 1.0, 0.0).astype(F32))
    dir_refs = ((rf_ref, vf_ref, kkf_ref, ldf_ref, ktf_ref, bbf_ref),
                (rb_ref, vb_ref, kkb_ref, ldb_ref, ktb_ref, bbb_ref))
    pa, pi, pm, ps, pr = (_RWKV_PREC[k] for k in ('a', 'inv', 'merge', 'solve', 'rest'))

    def pair_mm(xs, p, passes):
        return _mm_shared(xs, _pair_blockdiag(p, left), passes)

    def chunk(ci, carry):
        rows = (pl.ds(pl.multiple_of(ci * L, L), L), pl.ds(pl.multiple_of((nchunk - 1 - ci) * L, L), L))
        kap, bet, kti, rti, vp, g_last = [], [], [], [], [], []
        for d in range(2):
            r_ref, v_ref, kk_ref, ld_ref, kt_ref, bb_ref = dir_refs[d]
            ld = ld_ref[0, 0, rows[d], :]
            cs = _dot_hi(tri[d], ld)
            g_in = jnp.exp(cs)
            g_inv = jnp.exp(-cs)
            kap.append(_pair_split(kk_ref[0, rows[d], :] * jnp.exp(cs - ld)))
            bet.append(_pair_split(bb_ref[0, 0, rows[d], :] * g_inv))
            kti.append(_pair_split(kt_ref[0, 0, rows[d], :] * g_inv))
            rti.append(_pair_split(r_ref[0, rows[d], :] * g_in))
            vp.append(_pair_split(v_ref[0, rows[d], :]))
            last = 0 if d == 1 else L - 1
            g_last.append(_pair_split(g_in[last:last + 1, :]))
        kap, bet, kti, rti, vp, g_last = (jnp.concatenate(t, axis=0) for t in (kap, bet, kti, rti, vp, g_last))
        s_prev = s_s[...]
        x2 = jnp.concatenate([kap, rti], axis=1)
        a_b = _mm(x2, _pair_blockdiag(bet, left), _BNT, pa)
        a_k = _mm(x2, _pair_blockdiag(kti, left), _BNT, pa)
        p_s = _mm(x2, _pair_blockdiag(s_prev, left), _BNT, pr)
        a_ub = jnp.where(strict, a_b[:, :L], 0.0)
        a_uk = jnp.where(strict, a_k[:, :L], 0.0)
        a_rb = jnp.where(incl, a_b[:, L:], 0.0)
        a_rk = jnp.where(incl, a_k[:, L:], 0.0)
        nil = jnp.where(same16, -a_ub, 0.0)
        tinv = eye_p + nil
        (pw,) = pair_mm([nil], nil, pi)
        for _ in range(2):
            t_pw, pw2 = pair_mm([tinv, pw], pw, pi)
            tinv, pw = tinv + t_pw, pw2
        tinv = tinv + pair_mm([tinv], pw, pi)[0]
        for off in (off32, off64):
            (a_t,) = pair_mm([jnp.where(off, a_ub, 0.0)], tinv, pm)
            tinv = tinv - pair_mm([tinv], a_t, pm)[0]
        (a_v,) = pair_mm([a_uk], vp, pr)
        uh = -pair_mm([tinv], p_s[:, :L] + a_v, ps)[0]
        a_r = jnp.concatenate([a_rb, a_rk], axis=2)
        uv_bd = jnp.concatenate([_pair_blockdiag(uh, left), _pair_blockdiag(vp, left)], axis=1)
        y = p_s[:, L:] + _mm(a_r, uv_bd, _BNN, pr)
        uv = jnp.concatenate([uh, vp], axis=1)
        bk = jnp.concatenate([bet, kti], axis=1)
        for g in range(2 * npair):
            upd = _mm(uv[g], bk[g], _TN, pr)
            s_s[g] = (s_prev[g] + jnp.where(left, upd[:HEAD_DIM], upd[HEAD_DIM:])) * g_last[g]
        for p in range(npair):
            yf_ref[0, rows[0], p * LANES:(p + 1) * LANES] = y[p]
            yb_ref[0, rows[1], p * LANES:(p + 1) * LANES] = y[npair + p]
        return carry

    lax.fori_loop(0, nchunk, chunk, 0)

    @pl.when(pl.program_id(1) == pl.num_programs(1) - 1)
    def _():
        for g in range(s_s.shape[0]):
            pair = s_s[g]
            sf_ref[0, 2 * g] = pair[:, :HEAD_DIM]
            sf_ref[0, 2 * g + 1] = pair[:, HEAD_DIM:]


def _rwkv_scan(r, v, kk, ld, kt, bb, s0):
    b, n, _ = r.shape
    nblk = n // TOKEN_BLOCK
    npair = D_BRANCH // LANES
    s0p = _heads_flat(s0)
    fwd = lambda bb_, i: (bb_, i, 0)
    bwd = lambda bb_, i: (bb_, nblk - 1 - i, 0)
    fwd2 = lambda bb_, i: (0, bb_, i, 0)
    bwd2 = lambda bb_, i: (1, bb_, nblk - 1 - i, 0)
    tok = lambda im: pl.BlockSpec((1, TOKEN_BLOCK, D_BRANCH), im)
    tok2 = lambda im: pl.BlockSpec((1, 1, TOKEN_BLOCK, D_BRANCH), im)
    yf, yb, sfp = pl.pallas_call(
        _rwkv_kernel,
        out_shape=[jax.ShapeDtypeStruct((b, n, D_BRANCH), F32), jax.ShapeDtypeStruct((b, n, D_BRANCH), F32),
                   jax.ShapeDtypeStruct(s0p.shape, F32)],
        grid=(b, nblk),
        in_specs=[tok(fwd), tok(fwd), tok(fwd), tok2(fwd2), tok2(fwd2), tok2(fwd2),
                  tok(bwd), tok(bwd), tok(bwd), tok2(bwd2), tok2(bwd2), tok2(bwd2), _state_spec(s0p.shape)],
        out_specs=[tok(fwd), tok(bwd), _state_spec(s0p.shape)],
        scratch_shapes=[pltpu.VMEM((2 * npair, HEAD_DIM, LANES), F32)],
        compiler_params=_params("parallel", "arbitrary"),
        name="rwkv_scan",
    )(r, v, kk, ld, kt, bb, r, v, kk, ld, kt, bb, s0p)
    return yf, yb, sfp.reshape(s0.shape)


def _combine_kernel(x_ref, mod_ref, maf_ref, mab_ref, og_ref, lf_ref, lb_ref, lg_ref,
                    rf_ref, rb_ref, rg_ref, wf_ref, wb_ref, bonus_ref, wg_ref, ones_ref, wo_ref,
                    *rest, final):
    if final:
        fg_ref, o_ref = rest
    else:
        (o_ref,) = rest
    ones_bd = ones_ref[...]
    og = og_ref[0]
    h_a = (maf_ref[0] + mab_ref[0]) * _sigmoid(og[:, :D_BRANCH])
    y_a = _head_norm(h_a, ones_bd) * _silu(og[:, D_BRANCH:])
    y_b = (lf_ref[0] + lb_ref[0]) * _silu(lg_ref[0])
    y_c = _head_norm(rf_ref[0] + rb_ref[0], ones_bd) * _silu(rg_ref[0])
    y_d = (_head_norm(wf_ref[0] + wb_ref[0], ones_bd) + bonus_ref[0]) * _silu(wg_ref[0])
    y = (_dot(y_a, wo_ref[0:D_BRANCH, :]) + _dot(y_b, wo_ref[D_BRANCH:2 * D_BRANCH, :])
         + _dot(y_c, wo_ref[2 * D_BRANCH:3 * D_BRANCH, :]) + _dot(y_d, wo_ref[3 * D_BRANCH:, :]))
    x = x_ref[0] + mod_ref[0, 2:3, :] * y
    if final:
        x = x * lax.rsqrt(jnp.mean(x * x, axis=-1, keepdims=True) + EPS) * fg_ref[...]
    o_ref[0] = x


def _combine(x, mod, branch_arrays, ones_bd, w_out, final_g):
    b, n, _ = x.shape
    tm = TOKEN_BLOCK
    tok = lambda i, j: (i, j, 0)
    final = final_g is not None
    in_specs = [pl.BlockSpec((1, tm, D_MODEL), tok), pl.BlockSpec((1, 3, D_MODEL), lambda i, j: (i, 0, 0))]
    in_specs += [pl.BlockSpec((1, tm, a.shape[-1]), tok) for a in branch_arrays]
    in_specs += [_const_spec((D_BRANCH, D_BRANCH)), _const_spec(w_out.shape)]
    args = [x, mod, *branch_arrays, ones_bd, w_out]
    if final:
        in_specs.append(_const_spec((1, D_MODEL)))
        args.append(final_g)
    return pl.pallas_call(
        functools.partial(_combine_kernel, final=final),
        out_shape=jax.ShapeDtypeStruct(x.shape, F32),
        grid=(b, n // tm),
        in_specs=in_specs,
        out_specs=pl.BlockSpec((1, tm, D_MODEL), tok),
        compiler_params=_params("parallel", "parallel"),
        name="combine_outproj",
    )(*args)


def _prepare_weights(w_in, mlstm_gate_b, lru_gate_w):
    p_mlstm = 5 * D_BRANCH + 4 * N_HEADS
    a0, a1 = 0, p_mlstm
    b1 = a1 + 2 * D_BRANCH
    c1 = b1 + 4 * D_BRANCH
    wa = w_in[:, :, a0:a1]
    pad = jnp.zeros((DEPTH, D_MODEL, LANES - 4 * N_HEADS), w_in.dtype)
    w_a = jnp.concatenate([wa[:, :, :3 * D_BRANCH], wa[:, :, 5 * D_BRANCH:], pad,
                           wa[:, :, 3 * D_BRANCH:5 * D_BRANCH]], axis=-1).astype(BF16)
    w_b = w_in[:, :, a1:b1].astype(BF16)
    w_c = w_in[:, :, b1:c1].astype(BF16)
    w_d = w_in[:, :, c1:].astype(BF16)
    gate_b = jnp.pad(mlstm_gate_b, ((0, 0), (0, LANES - 4 * N_HEADS)))[:, None, :]
    eye_h = jnp.eye(N_HEADS, dtype=lru_gate_w.dtype)
    gw_bd = jnp.einsum('ldghij,hk->ldghikj', lru_gate_w, eye_h).reshape(
        DEPTH, 2, 2, D_BRANCH, D_BRANCH).astype(BF16)
    return w_a, w_b, w_c, w_d, gate_b, gw_bd


def _rope_tables(n):
    rows = n // GRID_W
    row_idx = jnp.broadcast_to(jnp.arange(rows, dtype=F32)[:, None], (rows, GRID_W)).reshape(-1)
    col_idx = jnp.broadcast_to(jnp.arange(GRID_W, dtype=F32)[None, :], (rows, GRID_W)).reshape(-1)
    n_freq = HEAD_DIM // 4
    freqs = ROPE_BASE ** (-jnp.arange(n_freq, dtype=F32) / n_freq)
    ang = jnp.concatenate([row_idx[:, None] * freqs, col_idx[:, None] * freqs], -1)
    cos, sin = jnp.cos(ang), jnp.sin(ang)
    cos_t = jnp.concatenate([cos, cos] * (LANES // HEAD_DIM), axis=-1)
    sin_t = jnp.concatenate([-sin, sin] * (LANES // HEAD_DIM), axis=-1)
    return cos_t, sin_t


def _layer(x, mod, lw, states, latent, rope_tabs, ones_bd, final_g):
    c0, n0, m0, h0, r0, s0 = states
    norm_g = lw['norm_g']
    u_a, og_a = _inproj(x, norm_g, mod, lw['w_a'], (3 * D_BRANCH + LANES, 2 * D_BRANCH))
    x_b, g_b = _inproj(x, norm_g, mod, lw['w_b'], (D_BRANCH, D_BRANCH))
    u_c, g_c = _inproj(x, norm_g, mod, lw['w_c'], (3 * D_BRANCH, D_BRANCH))
    s_d, g_d = _inproj(x, norm_g, mod, lw['w_d'], (RWKV_SHIFT, D_BRANCH))

    prep = _rwkv_prep(s_d, lw['rwkv_mu'], lw['rwkv_kk'], lw['rwkv_ka'], lw['rwkv_rk'], lw['rwkv_w0'],
                      lw['rwkv_w2'], lw['rwkv_a0'], lw['rwkv_a2'], ones_bd, latent)
    r_d, v_d, kk_d, bonus_d, ld_d, kt_d, bb_d = prep

    yaf, yab, c_new, n_new, m_new = _mlstm_scan(u_a, lw['gate_b'], c0, n0, m0)
    lru = [_lru_scan(x_b, lw['lru_conv_w'], lw['lru_conv_b'], lw['gw_bd'][d], lw['lru_gate_b'][d],
                     lw['lru_lambda'][d][None, :], h0[:, d][:, None, :], d) for d in range(2)]
    (ybf, hf), (ybb, hb) = lru
    ycf, ycb, r_new = _ret_scan(u_c, lw['theta'], r0, rope_tabs)
    ydf, ydb, s_new = _rwkv_scan(r_d, v_d, kk_d, ld_d, kt_d, bb_d, s0)
    branch_arrays = [yaf, yab, og_a, ybf, ybb, g_b, ycf, ycb, g_c, ydf, ydb, bonus_d, g_d]
    x_new = _combine(x, mod, branch_arrays, ones_bd, lw['w_out'], final_g)
    h_new = jnp.stack([hf[:, 0, :], hb[:, 0, :]], axis=1)
    return x_new, (c_new, n_new, m_new, h_new, r_new, s_new)


def kernel(x_prompt, x_sample, c, state_mlstm_c, state_mlstm_n, state_mlstm_m, state_lru_h, state_ret_r,
           state_rwkv_s, c_ctx, norm_g, w_mod, b_mod, w_in, w_out, mlstm_gate_b, lru_conv_w, lru_conv_b,
           lru_gate_w, lru_gate_b, lru_lambda, ret_theta, rwkv_mu, rwkv_w0, rwkv_w2, rwkv_a0, rwkv_a2,
           rwkv_kk, rwkv_ka, rwkv_rk, final_g):
    bp = x_prompt.shape[0]
    bs = x_sample.shape[0]
    assert x_prompt.shape[1] == TOKEN_BLOCK and x_sample.shape[1] % TOKEN_BLOCK == 0
    assert 1 + bs <= 8

    w_a, w_b, w_c, w_d, gate_b, gw_bd = _prepare_weights(w_in, mlstm_gate_b, lru_gate_w)
    w_out_bf = w_out.astype(BF16)
    head_of = jnp.arange(D_BRANCH) // HEAD_DIM
    ones_bd = (head_of[:, None] == head_of[None, :]).astype(BF16)
    theta = jnp.pad(ret_theta.reshape(DEPTH, 1, 2 * N_HEADS), ((0, 0), (0, 0), (0, LANES - 2 * N_HEADS)))
    rope_tabs = _rope_tables(x_sample.shape[1])

    cvec = jnp.concatenate([c_ctx[None, :], c, jnp.zeros((8 - 1 - bs, D_MODEL), F32)], axis=0)
    mods = _modulation(cvec, w_mod, b_mod).reshape(DEPTH, 8, 3, D_MODEL)

    def layer_weights(l):
        return dict(norm_g=norm_g[l][None, :], w_a=w_a[l], w_b=w_b[l], w_c=w_c[l], w_d=w_d[l],
                    gate_b=gate_b[l], gw_bd=gw_bd[l], lru_conv_w=lru_conv_w[l],
                    lru_conv_b=lru_conv_b[l][None, :], lru_gate_b=lru_gate_b[l], lru_lambda=lru_lambda[l],
                    theta=theta[l], rwkv_mu=rwkv_mu[l][None, :], rwkv_kk=rwkv_kk[l][None, :],
                    rwkv_ka=rwkv_ka[l][None, :], rwkv_rk=rwkv_rk[l][None, :], rwkv_w0=rwkv_w0[l],
                    rwkv_w2=rwkv_w2[l].astype(BF16), rwkv_a0=rwkv_a0[l], rwkv_a2=rwkv_a2[l].astype(BF16),
                    w_out=w_out_bf[l])

    zero_states = (jnp.zeros((bp, 2, N_HEADS, HEAD_DIM, HEAD_DIM), F32),
                   jnp.zeros((bp, 2, N_HEADS, HEAD_DIM), F32),
                   jnp.zeros((bp, 2, N_HEADS), F32),
                   jnp.zeros((bp, 2, D_BRANCH), F32),
                   jnp.zeros((bp, 2, N_HEADS, HEAD_DIM, HEAD_DIM), F32),
                   jnp.zeros((bp, 2, N_HEADS, HEAD_DIM, HEAD_DIM), F32))
    xp = x_prompt
    per_layer = []
    for l in range(DEPTH):
        mod = jnp.broadcast_to(mods[l, 0][None], (bp, 3, D_MODEL))
        xp, st = _layer(xp, mod, layer_weights(l), zero_states, False, None, ones_bd,
                        final_g[None, :] if l == DEPTH - 1 else None)
        per_layer.append(st)
    new_states = tuple(jnp.stack([st[j] for st in per_layer], axis=1) for j in range(6))

    xs = x_sample
    for l in range(DEPTH):
        mod = mods[l, 1:1 + bs]
        st = (state_mlstm_c[:, l], state_mlstm_n[:, l], state_mlstm_m[:, l], state_lru_h[:, l],
              state_ret_r[:, l], state_rwkv_s[:, l])
        xs, _ = _layer(xs, mod, layer_weights(l), st, True, rope_tabs, ones_bd,
                       final_g[None, :] if l == DEPTH - 1 else None)
    return (xp, xs) + new_states
```

```python
import functools

import jax
import jax.numpy as jnp
from jax import lax
from jax.experimental import pallas as pl
from jax.experimental.pallas import tpu as pltpu

F32 = jnp.float32
BF16 = jnp.bfloat16
HIGHEST = lax.Precision.HIGHEST

D_MODEL = 1024
DEPTH = 4
GRID_W = 64
D_BRANCH = 512
HEAD_DIM = 64
N_HEADS = 8
CHUNK = 64
CONV_W = 4
LRU_C = 8.0
LORA = 64
ROPE_BASE = 100.0
EPS = 1e-6
RWKV_SHIFT = 3 * D_BRANCH + 2 * LORA

LANES = 128
TOKEN_BLOCK = 256
INPROJ_BLOCK = 512
VMEM_LIMIT = 48 * 1024 * 1024
NEG = -1e30


def _sigmoid(x):
    return 1.0 / (1.0 + jnp.exp(-x))


def _silu(x):
    return x * _sigmoid(x)


def _softplus(x):
    return jnp.maximum(x, 0.0) + jnp.log(1.0 + jnp.exp(-jnp.abs(x)))


def _log_sigmoid(x):
    return -_softplus(-x)


def _dot(a, b):
    return jnp.dot(a.astype(BF16), b.astype(BF16), preferred_element_type=F32)


def _dot_tn(a, b):
    return lax.dot_general(a.astype(BF16), b.astype(BF16), (((0,), (0,)), ((), ())),
                           preferred_element_type=F32)


def _dot_hi(a, b):
    return jnp.dot(a, b, preferred_element_type=F32, precision=HIGHEST)


_NN = (((1,), (0,)), ((), ()))
_NT = (((1,), (1,)), ((), ()))
_TN = (((0,), (0,)), ((), ()))
_BNN = (((2,), (1,)), ((0,), (0,)))
_BNT = (((2,), (2,)), ((0,), (0,)))
_RWKV_PREC = dict(a=1, inv=3, merge=1, solve=3, rest=1)


def _mm(a, b, dims, passes):
    if passes == 6:
        return lax.dot_general(a, b, dims, preferred_element_type=F32, precision=HIGHEST)
    ah = a.astype(BF16)
    bh = b.astype(BF16)
    out = lax.dot_general(ah, bh, dims, preferred_element_type=F32)
    if passes == 3:
        al = (a - ah.astype(F32)).astype(BF16)
        bl = (b - bh.astype(F32)).astype(BF16)
        out = (out + lax.dot_general(ah, bl, dims, preferred_element_type=F32)
               + lax.dot_general(al, bh, dims, preferred_element_type=F32))
    return out


def _mm_shared(lhs, b, passes):
    n, m = len(lhs), lhs[0].shape[1]
    bh = b.astype(BF16)
    hs = [x.astype(BF16) for x in lhs]
    if passes == 1:
        out = lax.dot_general(jnp.concatenate(hs, axis=1), bh, _BNN, preferred_element_type=F32)
        return [out[:, i * m:(i + 1) * m] for i in range(n)]
    assert passes == 3
    ls = [(x - h.astype(F32)).astype(BF16) for x, h in zip(lhs, hs)]
    bl = (b - bh.astype(F32)).astype(BF16)
    o1 = lax.dot_general(jnp.concatenate(hs + ls, axis=1), bh, _BNN, preferred_element_type=F32)
    o2 = lax.dot_general(jnp.concatenate(hs, axis=1), bl, _BNN, preferred_element_type=F32)
    return [o1[:, i * m:(i + 1) * m] + o1[:, (n + i) * m:(n + i + 1) * m] + o2[:, i * m:(i + 1) * m]
            for i in range(n)]


def _head_sum(x, ones_bd):
    hi = x.astype(BF16)
    lo = (x - hi.astype(F32)).astype(BF16)
    return (jnp.dot(hi, ones_bd, preferred_element_type=F32)
            + jnp.dot(lo, ones_bd, preferred_element_type=F32))


def _head_norm(x, ones_bd):
    return x * lax.rsqrt(_head_sum(x * x, ones_bd) * (1.0 / HEAD_DIM) + EPS)


def _params(*sem):
    return pltpu.CompilerParams(dimension_semantics=sem, vmem_limit_bytes=VMEM_LIMIT)


def _blk_index(reverse, nblk):
    if reverse:
        return lambda b, i: (b, nblk - 1 - i, 0)
    return lambda b, i: (b, i, 0)


def _state_spec(shape):
    nd = len(shape)
    return pl.BlockSpec((1,) + tuple(shape[1:]), lambda b, i: (b,) + (0,) * (nd - 1))


def _const_spec(shape):
    nd = len(shape)
    return pl.BlockSpec(tuple(shape), lambda *_: (0,) * nd)


def _mod_kernel(c_ref, w_ref, b_ref, o_ref):
    sc = _silu(c_ref[...])
    o_ref[0] = _dot(sc, w_ref[0]) + b_ref[0]


def _modulation(cvec, w_mod, b_mod):
    nt = 3
    return pl.pallas_call(
        _mod_kernel,
        out_shape=jax.ShapeDtypeStruct((DEPTH, 8, 3 * D_MODEL), F32),
        grid=(DEPTH, nt),
        in_specs=[pl.BlockSpec((8, D_MODEL), lambda l, j: (0, 0)),
                  pl.BlockSpec((1, D_MODEL, D_MODEL), lambda l, j: (l, 0, j)),
                  pl.BlockSpec((1, 1, D_MODEL), lambda l, j: (l, 0, j))],
        out_specs=pl.BlockSpec((1, 8, D_MODEL), lambda l, j: (l, 0, j)),
        compiler_params=_params("arbitrary", "arbitrary"),
        name="modulation",
    )(cvec, w_mod, b_mod.reshape(DEPTH, 1, 3 * D_MODEL))


def _inproj_kernel(x_ref, g_ref, mod_ref, w_ref, *o_refs, widths):
    x = x_ref[0]
    y = x * lax.rsqrt(jnp.mean(x * x, axis=-1, keepdims=True) + EPS) * g_ref[...]
    h = y * (1.0 + mod_ref[0, 1:2, :]) + mod_ref[0, 0:1, :]
    u = _dot(h, w_ref[...])
    off = 0
    for o_ref, w in zip(o_refs, widths):
        o_ref[0] = u[:, off:off + w]
        off += w


def _inproj(x, g, mod, w, widths):
    b, n, _ = x.shape
    p = w.shape[1]
    tm = INPROJ_BLOCK
    assert n % tm == 0
    return pl.pallas_call(
        functools.partial(_inproj_kernel, widths=widths),
        out_shape=[jax.ShapeDtypeStruct((b, n, wd), F32) for wd in widths],
        grid=(b, n // tm),
        in_specs=[pl.BlockSpec((1, tm, D_MODEL), lambda i, j: (i, j, 0)),
                  _const_spec((1, D_MODEL)),
                  pl.BlockSpec((1, 3, D_MODEL), lambda i, j: (i, 0, 0)),
                  _const_spec((D_MODEL, p))],
        out_specs=[pl.BlockSpec((1, tm, wd), lambda i, j: (i, j, 0)) for wd in widths],
        compiler_params=_params("parallel", "parallel"),
        name="inproj",
    )(x, g, mod, w)


def _expand_exact(x, sel):
    m = x.shape[0]
    hi = x.astype(BF16)
    r1 = x - hi.astype(F32)
    mid = r1.astype(BF16)
    lo = (r1 - mid.astype(F32)).astype(BF16)
    out = jnp.dot(jnp.concatenate([hi, mid, lo], axis=0), sel, preferred_element_type=F32)
    return out[:m] + out[m:2 * m] + out[2 * m:]


def _mlstm_kernel(uf_ref, ub_ref, gb_ref, c0_ref, n0_ref, m0_ref, yf_ref, yb_ref, cf_ref, nf_ref, mf_ref,
                  c_s, n_s, m_s):
    @pl.when(pl.program_id(1) == 0)
    def _():
        _load_blockdiag(c0_ref, c_s)
        n_s[...] = n0_ref[0]
        m_s[...] = m0_ref[0]

    L = CHUNK
    nchunk = TOKEN_BLOCK // L
    npair = D_BRANCH // LANES
    lane = lax.broadcasted_iota(jnp.int32, (L, LANES), 1)
    left = lane < HEAD_DIM
    row_w = lax.broadcasted_iota(jnp.int32, (L, D_BRANCH), 0)
    col_w = lax.broadcasted_iota(jnp.int32, (L, D_BRANCH), 1) & (HEAD_DIM - 1)
    incl_w = (col_w <= row_w, col_w >= row_w)
    diag_w = col_w == row_w
    row_g = lax.broadcasted_iota(jnp.int32, (L, LANES), 0)
    row_l = lax.broadcasted_iota(jnp.int32, (L, L), 0)
    col_l = lax.broadcasted_iota(jnp.int32, (L, L), 1)
    tri = (jnp.where(col_l <= row_l, 1.0, 0.0).astype(F32), jnp.where(col_l >= row_l, 1.0, 0.0).astype(F32))
    sel_r = lax.broadcasted_iota(jnp.int32, (LANES, D_BRANCH), 0)
    sel_h = lax.broadcasted_iota(jnp.int32, (LANES, D_BRANCH), 1) >> 6
    sel_ig = [jnp.where(sel_r == 2 * N_HEADS * d + sel_h, 1.0, 0.0).astype(BF16) for d in range(2)]
    sel_fg = [jnp.where(sel_r == 2 * N_HEADS * d + N_HEADS + sel_h, 1.0, 0.0).astype(BF16) for d in range(2)]
    sq_r = lax.broadcasted_iota(jnp.int32, (LANES, LANES), 0) >> 6
    sq_c = lax.broadcasted_iota(jnp.int32, (LANES, LANES), 1) >> 6
    same_head = sq_r == sq_c
    ones_bd = jnp.where(same_head, 1.0, 0.0).astype(BF16)
    u_refs = (uf_ref, ub_ref)
    scale = HEAD_DIM ** -0.5

    def chunk(ci, carry):
        rows = (pl.ds(pl.multiple_of(ci * L, L), L), pl.ds(pl.multiple_of((nchunk - 1 - ci) * L, L), L))
        parts = []
        for d in range(2):
            u_ref = u_refs[d]
            last = 0 if d == 1 else L - 1
            gates = u_ref[0, rows[d], 3 * D_BRANCH:3 * D_BRANCH + LANES] + gb_ref[...]
            bmat = _dot_hi(tri[d], _log_sigmoid(gates))
            b_at_ig = pltpu.roll(bmat, LANES - N_HEADS, axis=1)
            cmx = gates - b_at_ig
            sh = 1
            while sh < L:
                if d == 0:
                    cmx = jnp.where(row_g >= sh, jnp.maximum(cmx, pltpu.roll(cmx, sh, axis=0)), cmx)
                else:
                    cmx = jnp.where(row_g < L - sh, jnp.maximum(cmx, pltpu.roll(cmx, L - sh, axis=0)), cmx)
                sh *= 2
            rmax = _expand_exact(b_at_ig + cmx, sel_ig[d])
            bcol = _expand_exact(bmat, sel_fg[d])
            igcol = _expand_exact(gates, sel_ig[d])
            crow = jnp.sum(jnp.where(diag_w, igcol - bcol, 0.0), axis=0, keepdims=True)
            m_prev = m_s[d]
            m_inter = bcol + m_prev
            m_t = jnp.maximum(m_inter, rmax)
            pexp = jnp.exp(jnp.where(incl_w[d], bcol + crow, NEG) - m_t)
            w_inter = jnp.exp(m_inter - m_t)
            em = jnp.exp(-m_t)
            b_last = bcol[last:last + 1, :]
            w_log = b_last - bcol + igcol
            m_new = jnp.maximum(b_last + m_prev, jnp.max(w_log, axis=0, keepdims=True))
            dec = jnp.exp(b_last + m_prev - m_new)
            wdec = jnp.exp(w_log - m_new)
            m_s[d] = m_new
            q = u_ref[0, rows[d], 0:D_BRANCH]
            k = u_ref[0, rows[d], D_BRANCH:2 * D_BRANCH] * scale
            v = u_ref[0, rows[d], 2 * D_BRANCH:3 * D_BRANCH]
            parts.append([_pair_split(t) for t in (q, k, v, pexp, w_inter, em, k * wdec, dec)])
        qp, kp, vp, pexp, w_inter, em, wk, dec = (jnp.concatenate(t, axis=0) for t in zip(*parts))
        c_prev = c_s[...]
        n_prev = n_s[...]
        s = _mm(qp, _pair_blockdiag(kp, left), _BNT, 1) * pexp
        num = _mm(s, _pair_blockdiag(vp, left), _BNN, 1) + w_inter * _mm(qp, c_prev, _BNN, 1)
        both = jnp.concatenate([s, qp * n_prev], axis=1).reshape(2 * npair * 2 * L, LANES)
        sums = _head_sum(both, ones_bd).reshape(2 * npair, 2 * L, LANES)
        den = sums[:, :L] + w_inter * sums[:, L:]
        hh = num / jnp.maximum(jnp.abs(den), em)
        n_s[...] = dec * n_prev + jnp.sum(wk, axis=1, keepdims=True)
        for g in range(2 * npair):
            c_s[g] = dec[g] * c_prev[g] + jnp.where(same_head, _dot_tn(wk[g], vp[g]), 0.0)
        for p in range(npair):
            yf_ref[0, rows[0], p * LANES:(p + 1) * LANES] = hh[p]
            yb_ref[0, rows[1], p * LANES:(p + 1) * LANES] = hh[npair + p]
        return carry

    lax.fori_loop(0, nchunk, chunk, 0)

    @pl.when(pl.program_id(1) == pl.num_programs(1) - 1)
    def _():
        _store_blockdiag(c_s, cf_ref)
        nf_ref[0] = n_s[...]
        mf_ref[0] = m_s[...]


def _mlstm_scan(u, gate_b, c0, n0, m0):
    b, n, p = u.shape
    nblk = n // TOKEN_BLOCK
    groups = 2 * (D_BRANCH // LANES)
    fwd = lambda bb, i: (bb, i, 0)
    bwd = lambda bb, i: (bb, nblk - 1 - i, 0)
    c0p = _heads_flat(c0)
    n0p = n0.reshape(b, groups, 1, LANES)
    m0p = jnp.repeat(m0, HEAD_DIM, axis=-1)[:, :, None, :]
    states = (c0p, n0p, m0p)
    yf, yb, cfp, nfp, mfp = pl.pallas_call(
        _mlstm_kernel,
        out_shape=[jax.ShapeDtypeStruct((b, n, D_BRANCH), F32), jax.ShapeDtypeStruct((b, n, D_BRANCH), F32)]
                  + [jax.ShapeDtypeStruct(s.shape, F32) for s in states],
        grid=(b, nblk),
        in_specs=[pl.BlockSpec((1, TOKEN_BLOCK, p), fwd), pl.BlockSpec((1, TOKEN_BLOCK, p), bwd),
                  _const_spec((1, LANES))] + [_state_spec(s.shape) for s in states],
        out_specs=[pl.BlockSpec((1, TOKEN_BLOCK, D_BRANCH), fwd), pl.BlockSpec((1, TOKEN_BLOCK, D_BRANCH), bwd)]
                  + [_state_spec(s.shape) for s in states],
        scratch_shapes=[pltpu.VMEM((groups, LANES, LANES), F32), pltpu.VMEM(n0p.shape[1:], F32),
                        pltpu.VMEM(m0p.shape[1:], F32)],
        compiler_params=_params("parallel", "arbitrary"),
        name="mlstm_scan",
    )(u, u, gate_b, *states)
    return yf, yb, cfp.reshape(c0.shape), nfp.reshape(n0.shape), mfp[:, :, 0, ::HEAD_DIM]


def _ret_kernel(*refs, rope):
    if rope:
        uf_ref, ub_ref, th_ref, cosf_ref, sinf_ref, cosb_ref, sinb_ref, r0_ref, yf_ref, yb_ref, rf_ref, r_s = refs
        tabs = ((cosf_ref, sinf_ref), (cosb_ref, sinb_ref))
    else:
        uf_ref, ub_ref, th_ref, r0_ref, yf_ref, yb_ref, rf_ref, r_s = refs
    u_refs = (uf_ref, ub_ref)

    @pl.when(pl.program_id(1) == 0)
    def _():
        _load_blockdiag(r0_ref, r_s)

    L = CHUNK
    nchunk = TOKEN_BLOCK // L
    npair = D_BRANCH // LANES
    row = lax.broadcasted_iota(jnp.int32, (L, LANES), 0)
    lane = lax.broadcasted_iota(jnp.int32, (L, LANES), 1)
    col = lane & (HEAD_DIM - 1)
    left = lane < HEAD_DIM
    left_sq = lax.broadcasted_iota(jnp.int32, (LANES, LANES), 1) < HEAD_DIM
    same_head = (lax.broadcasted_iota(jnp.int32, (LANES, LANES), 0) < HEAD_DIM) == left_sq

    def both(fwd, bwd):
        return jnp.concatenate([jnp.broadcast_to(fwd, (npair, L, LANES)),
                                jnp.broadcast_to(bwd, (npair, L, LANES))], axis=0)

    incl = both(col <= row, col >= row)
    diff = jnp.abs(row - col).astype(F32)
    rowf = row.astype(F32)
    p_pos = both(rowf, L - 1.0 - rowf)
    log_g = _log_sigmoid(th_ref[...])
    lg = jnp.stack([jnp.where(left[0:1], log_g[:, N_HEADS * d + 2 * p:N_HEADS * d + 2 * p + 1],
                              log_g[:, N_HEADS * d + 2 * p + 1:N_HEADS * d + 2 * p + 2])
                    for d in range(2) for p in range(npair)], axis=0)
    dmat = jnp.where(incl, jnp.exp(lg * diff), 0.0)
    xi = jnp.exp(lg * (p_pos + 1.0))
    wk = jnp.exp(lg * (L - 1.0 - p_pos))
    dec = jnp.exp(lg * float(L))
    lane_w = lax.broadcasted_iota(jnp.int32, (L, D_BRANCH), 1)
    first_half = (lane_w & (HEAD_DIM - 1)) < (HEAD_DIM // 2)

    def rot(x, c, s):
        swapped = jnp.where(first_half, pltpu.roll(x, D_BRANCH - HEAD_DIM // 2, axis=1),
                            pltpu.roll(x, HEAD_DIM // 2, axis=1))
        return x * c + swapped * s

    def chunk(ci, carry):
        rows = (pl.ds(pl.multiple_of(ci * L, L), L), pl.ds(pl.multiple_of((nchunk - 1 - ci) * L, L), L))
        qs, ks, vs = [], [], []
        for d in range(2):
            q = u_refs[d][0, rows[d], 0:D_BRANCH]
            k = u_refs[d][0, rows[d], D_BRANCH:2 * D_BRANCH]
            if rope:
                c = jnp.concatenate([tabs[d][0][rows[d], :]] * npair, axis=1)
                s = jnp.concatenate([tabs[d][1][rows[d], :]] * npair, axis=1)
                q = rot(q, c, s)
                k = rot(k, c, s)
            qs.append(_pair_split(q))
            ks.append(_pair_split(k * (HEAD_DIM ** -0.5)))
            vs.append(_pair_split(u_refs[d][0, rows[d], 2 * D_BRANCH:3 * D_BRANCH]))
        qp, kp, vp = (jnp.concatenate(t, axis=0) for t in (qs, ks, vs))
        r_prev = r_s[...]
        s_mat = _mm(qp, _pair_blockdiag(kp, left), _BNT, 1) * dmat
        y = _mm(s_mat, _pair_blockdiag(vp, left), _BNN, 1) + xi * _mm(qp, r_prev, _BNN, 1)
        kw = kp * wk
        for g in range(2 * npair):
            r_s[g] = dec[g] * r_prev[g] + jnp.where(same_head, _dot_tn(kw[g], vp[g]), 0.0)
        for p in range(npair):
            yf_ref[0, rows[0], p * LANES:(p + 1) * LANES] = y[p]
            yb_ref[0, rows[1], p * LANES:(p + 1) * LANES] = y[npair + p]
        return carry

    lax.fori_loop(0, nchunk, chunk, 0)

    @pl.when(pl.program_id(1) == pl.num_programs(1) - 1)
    def _():
        _store_blockdiag(r_s, rf_ref)


def _load_blockdiag(src_ref, dst_s):
    zero = jnp.zeros((HEAD_DIM, HEAD_DIM), F32)
    for g in range(dst_s.shape[0]):
        top = jnp.concatenate([src_ref[0, 2 * g], zero], axis=1)
        bot = jnp.concatenate([zero, src_ref[0, 2 * g + 1]], axis=1)
        dst_s[g] = jnp.concatenate([top, bot], axis=0)


def _store_blockdiag(src_s, dst_ref):
    for g in range(src_s.shape[0]):
        blk = src_s[g]
        dst_ref[0, 2 * g] = blk[:HEAD_DIM, :HEAD_DIM]
        dst_ref[0, 2 * g + 1] = blk[HEAD_DIM:, HEAD_DIM:]


def _heads_flat(s):
    return s.reshape(s.shape[0], 2 * N_HEADS, HEAD_DIM, HEAD_DIM)


def _ret_scan(u, theta, r0, rope_tabs):
    b, n, p = u.shape
    nblk = n // TOKEN_BLOCK
    fwd = lambda bb, i: (bb, i, 0)
    bwd = lambda bb, i: (bb, nblk - 1 - i, 0)
    rope = rope_tabs is not None
    r0p = _heads_flat(r0)
    in_specs = [pl.BlockSpec((1, TOKEN_BLOCK, p), fwd), pl.BlockSpec((1, TOKEN_BLOCK, p), bwd),
                _const_spec((1, LANES))]
    args = [u, u, theta]
    if rope:
        in_specs += [pl.BlockSpec((TOKEN_BLOCK, LANES), lambda bb, i: (i, 0))] * 2
        in_specs += [pl.BlockSpec((TOKEN_BLOCK, LANES), lambda bb, i: (nblk - 1 - i, 0))] * 2
        args += list(rope_tabs) * 2
    in_specs.append(_state_spec(r0p.shape))
    args.append(r0p)
    yf, yb, rfp = pl.pallas_call(
        functools.partial(_ret_kernel, rope=rope),
        out_shape=[jax.ShapeDtypeStruct((b, n, D_BRANCH), F32), jax.ShapeDtypeStruct((b, n, D_BRANCH), F32),
                   jax.ShapeDtypeStruct(r0p.shape, F32)],
        grid=(b, nblk),
        in_specs=in_specs,
        out_specs=[pl.BlockSpec((1, TOKEN_BLOCK, D_BRANCH), fwd), pl.BlockSpec((1, TOKEN_BLOCK, D_BRANCH), bwd),
                   _state_spec(r0p.shape)],
        scratch_shapes=[pltpu.VMEM((2 * (D_BRANCH // LANES), LANES, LANES), F32)],
        compiler_params=_params("parallel", "arbitrary"),
        name="retention_scan",
    )(*args)
    return yf, yb, rfp.reshape(r0.shape)


def _lru_kernel(x_ref, xp_ref, xn_ref, cw_ref, cb_ref, gw_ref, gb_ref, lam_ref, h0_ref,
                y_ref, hf_ref, h_s, *, reverse):
    i = pl.program_id(1)
    nblk = pl.num_programs(1)
    blk = (nblk - 1 - i) if reverse else i

    @pl.when(i == 0)
    def _():
        h_s[...] = h0_ref[0]

    tb = TOKEN_BLOCK
    x = x_ref[0]
    row = lax.broadcasted_iota(jnp.int32, (tb, D_BRANCH), 0)
    prev_ok = jnp.where(blk > 0, 1.0, 0.0)
    next_ok = jnp.where(blk < nblk - 1, 1.0, 0.0)
    p_last = xp_ref[0, 7:8, :] * prev_ok
    n_0 = xn_ref[0, 0:1, :] * next_ok
    n_1 = xn_ref[0, 1:2, :] * next_ok
    xm1 = jnp.where(row == 0, p_last, pltpu.roll(x, 1, axis=0))
    xp1 = jnp.where(row == tb - 1, n_0, pltpu.roll(x, tb - 1, axis=0))
    xp2 = jnp.where(row == tb - 1, n_1, jnp.where(row == tb - 2, n_0, pltpu.roll(x, tb - 2, axis=0)))
    xc = (cw_ref[0:1, :] * xm1 + cw_ref[1:2, :] * x + cw_ref[2:3, :] * xp1 + cw_ref[3:4, :] * xp2
          + cb_ref[...])
    gr = _dot(xc, gw_ref[0]) + gb_ref[0:1, :]
    gi = _dot(xc, gw_ref[1]) + gb_ref[1:2, :]
    log_a = -LRU_C * _sigmoid(gr) * _softplus(-lam_ref[...])
    a = jnp.exp(log_a)
    beta = jnp.sqrt(1.0 - a * a)
    bx = beta * _sigmoid(gi) * xc

    sh = 1
    while sh < tb:
        if sh < 8:
            if reverse:
                ok = row < tb - sh
                a_sh = pltpu.roll(a, tb - sh, axis=0)
                b_sh = pltpu.roll(bx, tb - sh, axis=0)
            else:
                ok = row >= sh
                a_sh = pltpu.roll(a, sh, axis=0)
                b_sh = pltpu.roll(bx, sh, axis=0)
            bx = jnp.where(ok, a * b_sh + bx, bx)
            a = jnp.where(ok, a * a_sh, a)
        elif reverse:
            bx = jnp.concatenate([a[:tb - sh] * bx[sh:] + bx[:tb - sh], bx[tb - sh:]], axis=0)
            a = jnp.concatenate([a[:tb - sh] * a[sh:], a[tb - sh:]], axis=0)
        else:
            bx = jnp.concatenate([bx[:sh], a[sh:] * bx[:tb - sh] + bx[sh:]], axis=0)
            a = jnp.concatenate([a[:sh], a[sh:] * a[:tb - sh]], axis=0)
        sh *= 2
    h = a * h_s[...] + bx
    y_ref[0] = h
    h_s[...] = h[0:1, :] if reverse else h[tb - 1:tb, :]

    @pl.when(i == nblk - 1)
    def _():
        hf_ref[0] = h_s[...]


def _lru_scan(xb, conv_w, conv_b, gate_w_bd, gate_b, lam, h0, direction):
    b, n, _ = xb.shape
    tb = TOKEN_BLOCK
    nblk = n // tb
    reverse = direction == 1
    idx = _blk_index(reverse, nblk)
    sub = tb // 8
    nsub = n // 8
    if reverse:
        prev_idx = lambda bb, i: (bb, jnp.maximum((nblk - 1 - i) * sub - 1, 0), 0)
        next_idx = lambda bb, i: (bb, jnp.minimum((nblk - i) * sub, nsub - 1), 0)
    else:
        prev_idx = lambda bb, i: (bb, jnp.maximum(i * sub - 1, 0), 0)
        next_idx = lambda bb, i: (bb, jnp.minimum((i + 1) * sub, nsub - 1), 0)
    return pl.pallas_call(
        functools.partial(_lru_kernel, reverse=reverse),
        out_shape=[jax.ShapeDtypeStruct((b, n, D_BRANCH), F32), jax.ShapeDtypeStruct(h0.shape, F32)],
        grid=(b, nblk),
        in_specs=[pl.BlockSpec((1, tb, D_BRANCH), idx),
                  pl.BlockSpec((1, 8, D_BRANCH), prev_idx),
                  pl.BlockSpec((1, 8, D_BRANCH), next_idx),
                  _const_spec((CONV_W, D_BRANCH)), _const_spec((1, D_BRANCH)),
                  _const_spec((2, D_BRANCH, D_BRANCH)), _const_spec((2, D_BRANCH)),
                  _const_spec((1, D_BRANCH)), _state_spec(h0.shape)],
        out_specs=[pl.BlockSpec((1, tb, D_BRANCH), idx), _state_spec(h0.shape)],
        scratch_shapes=[pltpu.VMEM((1, D_BRANCH), F32)],
        compiler_params=_params("parallel", "arbitrary"),
        name="rglru_scan",
    )(xb, xb, xb, conv_w, conv_b, gate_w_bd, gate_b, lam, h0)


def _rwkv_prep_kernel(s_ref, sp_ref, sn_ref, mu_ref, kk_ref, ka_ref, rk_ref, w0_ref, w2_ref,
                      a0_ref, a2_ref, ones_ref,
                      r_ref, v_ref, kkn_ref, bonus_ref, ld_ref, kt_ref, bb_ref, *, grid_shift):
    tb = TOKEN_BLOCK
    s = s_ref[0]
    width = s.shape[1]
    row = lax.broadcasted_iota(jnp.int32, (tb, width), 0)
    lane = lax.broadcasted_iota(jnp.int32, (tb, width), 1)
    if grid_shift:
        i = pl.program_id(1)
        nblk = pl.num_programs(1)
        col = row & (GRID_W - 1)
        qc = width // 4
        up_halo = sp_ref[0] * jnp.where(i > 0, 1.0, 0.0)
        dn_halo = sn_ref[0] * jnp.where(i < nblk - 1, 1.0, 0.0)
        left = jnp.where(col == 0, 0.0, pltpu.roll(s, 1, axis=0))
        right = jnp.where(col == GRID_W - 1, 0.0, pltpu.roll(s, tb - 1, axis=0))
        up = jnp.concatenate([up_halo, s[:tb - GRID_W]], axis=0)
        down = jnp.concatenate([s[GRID_W:], dn_halo], axis=0)
        sh = jnp.where(lane < qc, left,
                       jnp.where(lane < 2 * qc, right, jnp.where(lane < 3 * qc, up, down)))
    else:
        prev = jnp.where(row == 0, 0.0, pltpu.roll(s, 1, axis=0))
        nxt = jnp.where(row == tb - 1, 0.0, pltpu.roll(s, tb - 1, axis=0))
        sh = jnp.where(lane < width // 2, prev, nxt)
    s = s + mu_ref[...] * (sh - s)
    r = s[:, 0:D_BRANCH]
    k = s[:, D_BRANCH:2 * D_BRANCH]
    v = s[:, 2 * D_BRANCH:3 * D_BRANCH]
    lw = jnp.tanh(s[:, 3 * D_BRANCH:3 * D_BRANCH + LORA])
    la = s[:, 3 * D_BRANCH + LORA:3 * D_BRANCH + 2 * LORA]
    ones_bd = ones_ref[...]
    kkh = k * kk_ref[...]
    kk = kkh / jnp.maximum(jnp.sqrt(_head_sum(kkh * kkh, ones_bd)), 1e-12)
    r_ref[0] = r
    v_ref[0] = v
    kkn_ref[0] = kk
    bonus_ref[0] = _head_sum(r * k * rk_ref[...], ones_bd) * v
    for d in range(2):
        w_log = -_softplus(-(w0_ref[d:d + 1, :] + _dot(lw, w2_ref[d]))) - 0.5
        ld_ref[d, 0] = -jnp.exp(w_log)
        a = _sigmoid(a0_ref[d:d + 1, :] + _dot(la, a2_ref[d]))
        kt_ref[d, 0] = k * (1.0 + (a - 1.0) * ka_ref[...])
        bb_ref[d, 0] = kk * a


def _rwkv_prep(s, mu, k_k, k_a, r_k, w0, w2, a0, a2, ones_bd, grid_shift):
    b, n, width = s.shape
    tb = TOKEN_BLOCK
    nblk = n // tb
    assert grid_shift or nblk == 1
    halo = GRID_W
    sub = tb // halo
    nsub = n // halo
    tok = lambda bb, i: (bb, i, 0)
    tok2 = lambda bb, i: (0, bb, i, 0)
    out1 = jax.ShapeDtypeStruct((b, n, D_BRANCH), F32)
    out2 = jax.ShapeDtypeStruct((2, b, n, D_BRANCH), F32)
    return pl.pallas_call(
        functools.partial(_rwkv_prep_kernel, grid_shift=grid_shift),
        out_shape=[out1, out1, out1, out1, out2, out2, out2],
        grid=(b, nblk),
        in_specs=[pl.BlockSpec((1, tb, width), tok),
                  pl.BlockSpec((1, halo, width), lambda bb, i: (bb, jnp.maximum(i * sub - 1, 0), 0)),
                  pl.BlockSpec((1, halo, width), lambda bb, i: (bb, jnp.minimum((i + 1) * sub, nsub - 1), 0)),
                  _const_spec((1, width)), _const_spec((1, D_BRANCH)), _const_spec((1, D_BRANCH)),
                  _const_spec((1, D_BRANCH)), _const_spec((2, D_BRANCH)),
                  _const_spec((2, LORA, D_BRANCH)), _const_spec((2, D_BRANCH)),
                  _const_spec((2, LORA, D_BRANCH)), _const_spec((D_BRANCH, D_BRANCH))],
        out_specs=[pl.BlockSpec((1, tb, D_BRANCH), tok)] * 4
                  + [pl.BlockSpec((2, 1, tb, D_BRANCH), tok2)] * 3,
        compiler_params=_params("parallel", "parallel"),
        name="rwkv_prep",
    )(s, s, s, mu, k_k, k_a, r_k, w0, w2, a0, a2, ones_bd)


def _pair_split(x):
    return jnp.stack([x[:, p * LANES:(p + 1) * LANES] for p in range(D_BRANCH // LANES)], axis=0)


def _pair_blockdiag(p, left):
    return jnp.concatenate([jnp.where(left, p, 0.0), jnp.where(left, 0.0, p)], axis=1)


def _rwkv_kernel(rf_ref, vf_ref, kkf_ref, ldf_ref, ktf_ref, bbf_ref,
                 rb_ref, vb_ref, kkb_ref, ldb_ref, ktb_ref, bbb_ref, s0_ref,
                 yf_ref, yb_ref, sf_ref, s_s):
    @pl.when(pl.program_id(1) == 0)
    def _():
        for g in range(s_s.shape[0]):
            s_s[g] = jnp.concatenate([s0_ref[0, 2 * g], s0_ref[0, 2 * g + 1]], axis=1)

    L = CHUNK
    nchunk = TOKEN_BLOCK // L
    npair = D_BRANCH // LANES
    row = lax.broadcasted_iota(jnp.int32, (L, LANES), 0)
    lane = lax.broadcasted_iota(jnp.int32, (L, LANES), 1)
    col = lane & (HEAD_DIM - 1)
    left = lane < HEAD_DIM

    def both(fwd, bwd):
        return jnp.concatenate([jnp.broadcast_to(fwd, (npair, L, LANES)),
                                jnp.broadcast_to(bwd, (npair, L, LANES))], axis=0)

    incl = both(col <= row, col >= row)
    strict = both(col < row, col > row)
    eye_p = jnp.where(col == row, 1.0, 0.0).astype(F32)
    same16 = (row >> 4) == (col >> 4)
    same32 = (row >> 5) == (col >> 5)
    off32 = jnp.logical_and(same32, jnp.logical_not(same16))
    off64 = jnp.logical_not(same32)
    row_l = lax.broadcasted_iota(jnp.int32, (L, L), 0)
    col_l = lax.broadcasted_iota(jnp.int32, (L, L), 1)
    tri = (jnp.where(col_l <= row_l, 1.0, 0.0).astype(F32), jnp.where(col_l >= row_l, 1.0, 0.0).astype(F32))
    dir_refs = ((rf_ref, vf_ref, kkf_ref, ldf_ref, ktf_ref, bbf_ref),
                (rb_ref, vb_ref, kkb_ref, ldb_ref, ktb_ref, bbb_ref))
    pa, pi, pm, ps, pr = (_RWKV_PREC[k] for k in ('a', 'inv', 'merge', 'solve', 'rest'))

    def pair_mm(xs, p, passes):
        return _mm_shared(xs, _pair_blockdiag(p, left), passes)

    def chunk(ci, carry):
        rows = (pl.ds(pl.multiple_of(ci * L, L), L), pl.ds(pl.multiple_of((nchunk - 1 - ci) * L, L), L))
        kap, bet, kti, rti, vp, g_last = [], [], [], [], [], []
        for d in range(2):
            r_ref, v_ref, kk_ref, ld_ref, kt_ref, bb_ref = dir_refs[d]
            ld = ld_ref[0, 0, rows[d], :]
            cs = _dot_hi(tri[d], ld)
            g_in = jnp.exp(cs)
            g_inv = jnp.exp(-cs)
            kap.append(_pair_split(kk_ref[0, rows[d], :] * jnp.exp(cs - ld)))
            bet.append(_pair_split(bb_ref[0, 0, rows[d], :] * g_inv))
            kti.append(_pair_split(kt_ref[0, 0, rows[d], :] * g_inv))
            rti.append(_pair_split(r_ref[0, rows[d], :] * g_in))
            vp.append(_pair_split(v_ref[0, rows[d], :]))
            last = 0 if d == 1 else L - 1
            g_last.append(_pair_split(g_in[last:last + 1, :]))
        kap, bet, kti, rti, vp, g_last = (jnp.concatenate(t, axis=0) for t in (kap, bet, kti, rti, vp, g_last))
        s_prev = s_s[...]
        x2 = jnp.concatenate([kap, rti], axis=1)
        a_b = _mm(x2, _pair_blockdiag(bet, left), _BNT, pa)
        a_k = _mm(x2, _pair_blockdiag(kti, left), _BNT, pa)
        p_s = _mm(x2, _pair_blockdiag(s_prev, left), _BNT, pr)
        a_ub = jnp.where(strict, a_b[:, :L], 0.0)
        a_uk = jnp.where(strict, a_k[:, :L], 0.0)
        a_rb = jnp.where(incl, a_b[:, L:], 0.0)
        a_rk = jnp.where(incl, a_k[:, L:], 0.0)
        nil = jnp.where(same16, -a_ub, 0.0)
        tinv = eye_p + nil
        (pw,) = pair_mm([nil], nil, pi)
        for _ in range(2):
            t_pw, pw2 = pair_mm([tinv, pw], pw, pi)
            tinv, pw = tinv + t_pw, pw2
        tinv = tinv + pair_mm([tinv], pw, pi)[0]
        for off in (off32, off64):
            (a_t,) = pair_mm([jnp.where(off, a_ub, 0.0)], tinv, pm)
            tinv = tinv - pair_mm([tinv], a_t, pm)[0]
        (a_v,) = pair_mm([a_uk], vp, pr)
        uh = -pair_mm([tinv], p_s[:, :L] + a_v, ps)[0]
        a_r = jnp.concatenate([a_rb, a_rk], axis=2)
        uv_bd = jnp.concatenate([_pair_blockdiag(uh, left), _pair_blockdiag(vp, left)], axis=1)
        y = p_s[:, L:] + _mm(a_r, uv_bd, _BNN, pr)
        uv = jnp.concatenate([uh, vp], axis=1)
        bk = jnp.concatenate([bet, kti], axis=1)
        for g in range(2 * npair):
            upd = _mm(uv[g], bk[g], _TN, pr)
            s_s[g] = (s_prev[g] + jnp.where(left, upd[:HEAD_DIM], upd[HEAD_DIM:])) * g_last[g]
        for p in range(npair):
            yf_ref[0, rows[0], p * LANES:(p + 1) * LANES] = y[p]
            yb_ref[0, rows[1], p * LANES:(p + 1) * LANES] = y[npair + p]
        return carry

    lax.fori_loop(0, nchunk, chunk, 0)

    @pl.when(pl.program_id(1) == pl.num_programs(1) - 1)
    def _():
        for g in range(s_s.shape[0]):
            pair = s_s[g]
            sf_ref[0, 2 * g] = pair[:, :HEAD_DIM]
            sf_ref[0, 2 * g + 1] = pair[:, HEAD_DIM:]


def _rwkv_scan(r, v, kk, ld, kt, bb, s0):
    b, n, _ = r.shape
    nblk = n // TOKEN_BLOCK
    npair = D_BRANCH // LANES
    s0p = _heads_flat(s0)
    fwd = lambda bb_, i: (bb_, i, 0)
    bwd = lambda bb_, i: (bb_, nblk - 1 - i, 0)
    fwd2 = lambda bb_, i: (0, bb_, i, 0)
    bwd2 = lambda bb_, i: (1, bb_, nblk - 1 - i, 0)
    tok = lambda im: pl.BlockSpec((1, TOKEN_BLOCK, D_BRANCH), im)
    tok2 = lambda im: pl.BlockSpec((1, 1, TOKEN_BLOCK, D_BRANCH), im)
    yf, yb, sfp = pl.pallas_call(
        _rwkv_kernel,
        out_shape=[jax.ShapeDtypeStruct((b, n, D_BRANCH), F32), jax.ShapeDtypeStruct((b, n, D_BRANCH), F32),
                   jax.ShapeDtypeStruct(s0p.shape, F32)],
        grid=(b, nblk),
        in_specs=[tok(fwd), tok(fwd), tok(fwd), tok2(fwd2), tok2(fwd2), tok2(fwd2),
                  tok(bwd), tok(bwd), tok(bwd), tok2(bwd2), tok2(bwd2), tok2(bwd2), _state_spec(s0p.shape)],
        out_specs=[tok(fwd), tok(bwd), _state_spec(s0p.shape)],
        scratch_shapes=[pltpu.VMEM((2 * npair, HEAD_DIM, LANES), F32)],
        compiler_params=_params("parallel", "arbitrary"),
        name="rwkv_scan",
    )(r, v, kk, ld, kt, bb, r, v, kk, ld, kt, bb, s0p)
    return yf, yb, sfp.reshape(s0.shape)


def _combine_kernel(x_ref, mod_ref, maf_ref, mab_ref, og_ref, lf_ref, lb_ref, lg_ref,
                    rf_ref, rb_ref, rg_ref, wf_ref, wb_ref, bonus_ref, wg_ref, ones_ref, wo_ref,
                    *rest, final):
    if final:
        fg_ref, o_ref = rest
    else:
        (o_ref,) = rest
    ones_bd = ones_ref[...]
    og = og_ref[0]
    h_a = (maf_ref[0] + mab_ref[0]) * _sigmoid(og[:, :D_BRANCH])
    y_a = _head_norm(h_a, ones_bd) * _silu(og[:, D_BRANCH:])
    y_b = (lf_ref[0] + lb_ref[0]) * _silu(lg_ref[0])
    y_c = _head_norm(rf_ref[0] + rb_ref[0], ones_bd) * _silu(rg_ref[0])
    y_d = (_head_norm(wf_ref[0] + wb_ref[0], ones_bd) + bonus_ref[0]) * _silu(wg_ref[0])
    y = (_dot(y_a, wo_ref[0:D_BRANCH, :]) + _dot(y_b, wo_ref[D_BRANCH:2 * D_BRANCH, :])
         + _dot(y_c, wo_ref[2 * D_BRANCH:3 * D_BRANCH, :]) + _dot(y_d, wo_ref[3 * D_BRANCH:, :]))
    x = x_ref[0] + mod_ref[0, 2:3, :] * y
    if final:
        x = x * lax.rsqrt(jnp.mean(x * x, axis=-1, keepdims=True) + EPS) * fg_ref[...]
    o_ref[0] = x


def _combine(x, mod, branch_arrays, ones_bd, w_out, final_g):
    b, n, _ = x.shape
    tm = TOKEN_BLOCK
    tok = lambda i, j: (i, j, 0)
    final = final_g is not None
    in_specs = [pl.BlockSpec((1, tm, D_MODEL), tok), pl.BlockSpec((1, 3, D_MODEL), lambda i, j: (i, 0, 0))]
    in_specs += [pl.BlockSpec((1, tm, a.shape[-1]), tok) for a in branch_arrays]
    in_specs += [_const_spec((D_BRANCH, D_BRANCH)), _const_spec(w_out.shape)]
    args = [x, mod, *branch_arrays, ones_bd, w_out]
    if final:
        in_specs.append(_const_spec((1, D_MODEL)))
        args.append(final_g)
    return pl.pallas_call(
        functools.partial(_combine_kernel, final=final),
        out_shape=jax.ShapeDtypeStruct(x.shape, F32),
        grid=(b, n // tm),
        in_specs=in_specs,
        out_specs=pl.BlockSpec((1, tm, D_MODEL), tok),
        compiler_params=_params("parallel", "parallel"),
        name="combine_outproj",
    )(*args)


def _prepare_weights(w_in, mlstm_gate_b, lru_gate_w):
    p_mlstm = 5 * D_BRANCH + 4 * N_HEADS
    a0, a1 = 0, p_mlstm
    b1 = a1 + 2 * D_BRANCH
    c1 = b1 + 4 * D_BRANCH
    wa = w_in[:, :, a0:a1]
    pad = jnp.zeros((DEPTH, D_MODEL, LANES - 4 * N_HEADS), w_in.dtype)
    w_a = jnp.concatenate([wa[:, :, :3 * D_BRANCH], wa[:, :, 5 * D_BRANCH:], pad,
                           wa[:, :, 3 * D_BRANCH:5 * D_BRANCH]], axis=-1).astype(BF16)
    w_b = w_in[:, :, a1:b1].astype(BF16)
    w_c = w_in[:, :, b1:c1].astype(BF16)
    w_d = w_in[:, :, c1:].astype(BF16)
    gate_b = jnp.pad(mlstm_gate_b, ((0, 0), (0, LANES - 4 * N_HEADS)))[:, None, :]
    eye_h = jnp.eye(N_HEADS, dtype=lru_gate_w.dtype)
    gw_bd = jnp.einsum('ldghij,hk->ldghikj', lru_gate_w, eye_h).reshape(
        DEPTH, 2, 2, D_BRANCH, D_BRANCH).astype(BF16)
    return w_a, w_b, w_c, w_d, gate_b, gw_bd


def _rope_tables(n):
    rows = n // GRID_W
    row_idx = jnp.broadcast_to(jnp.arange(rows, dtype=F32)[:, None], (rows, GRID_W)).reshape(-1)
    col_idx = jnp.broadcast_to(jnp.arange(GRID_W, dtype=F32)[None, :], (rows, GRID_W)).reshape(-1)
    n_freq = HEAD_DIM // 4
    freqs = ROPE_BASE ** (-jnp.arange(n_freq, dtype=F32) / n_freq)
    ang = jnp.concatenate([row_idx[:, None] * freqs, col_idx[:, None] * freqs], -1)
    cos, sin = jnp.cos(ang), jnp.sin(ang)
    cos_t = jnp.concatenate([cos, cos] * (LANES // HEAD_DIM), axis=-1)
    sin_t = jnp.concatenate([-sin, sin] * (LANES // HEAD_DIM), axis=-1)
    return cos_t, sin_t


def _layer(x, mod, lw, states, latent, rope_tabs, ones_bd, final_g):
    c0, n0, m0, h0, r0, s0 = states
    norm_g = lw['norm_g']
    bsz, n, _ = x.shape
    x_in, mod_in = (x, mod) if latent else (x.reshape(1, bsz * n, D_MODEL), mod[:1])

    def project(w, widths):
        return [u.reshape(bsz, n, u.shape[-1]) for u in _inproj(x_in, norm_g, mod_in, w, widths)]

    u_a, og_a = project(lw['w_a'], (3 * D_BRANCH + LANES, 2 * D_BRANCH))
    x_b, g_b = project(lw['w_b'], (D_BRANCH, D_BRANCH))
    u_c, g_c = project(lw['w_c'], (3 * D_BRANCH, D_BRANCH))
    s_d, g_d = project(lw['w_d'], (RWKV_SHIFT, D_BRANCH))

    prep = _rwkv_prep(s_d, lw['rwkv_mu'], lw['rwkv_kk'], lw['rwkv_ka'], lw['rwkv_rk'], lw['rwkv_w0'],
                      lw['rwkv_w2'], lw['rwkv_a0'], lw['rwkv_a2'], ones_bd, latent)
    r_d, v_d, kk_d, bonus_d, ld_d, kt_d, bb_d = prep

    yaf, yab, c_new, n_new, m_new = _mlstm_scan(u_a, lw['gate_b'], c0, n0, m0)
    lru = [_lru_scan(x_b, lw['lru_conv_w'], lw['lru_conv_b'], lw['gw_bd'][d], lw['lru_gate_b'][d],
                     lw['lru_lambda'][d][None, :], h0[:, d][:, None, :], d) for d in range(2)]
    (ybf, hf), (ybb, hb) = lru
    ycf, ycb, r_new = _ret_scan(u_c, lw['theta'], r0, rope_tabs)
    ydf, ydb, s_new = _rwkv_scan(r_d, v_d, kk_d, ld_d, kt_d, bb_d, s0)
    branch_arrays = [yaf, yab, og_a, ybf, ybb, g_b, ycf, ycb, g_c, ydf, ydb, bonus_d, g_d]
    x_new = _combine(x, mod, branch_arrays, ones_bd, lw['w_out'], final_g)
    h_new = jnp.stack([hf[:, 0, :], hb[:, 0, :]], axis=1)
    return x_new, (c_new, n_new, m_new, h_new, r_new, s_new)


def kernel(x_prompt, x_sample, c, state_mlstm_c, state_mlstm_n, state_mlstm_m, state_lru_h, state_ret_r,
           state_rwkv_s, c_ctx, norm_g, w_mod, b_mod, w_in, w_out, mlstm_gate_b, lru_conv_w, lru_conv_b,
           lru_gate_w, lru_gate_b, lru_lambda, ret_theta, rwkv_mu, rwkv_w0, rwkv_w2, rwkv_a0, rwkv_a2,
           rwkv_kk, rwkv_ka, rwkv_rk, final_g):
    bp = x_prompt.shape[0]
    bs = x_sample.shape[0]
    assert x_prompt.shape[1] == TOKEN_BLOCK and x_sample.shape[1] % TOKEN_BLOCK == 0
    assert 1 + bs <= 8

    w_a, w_b, w_c, w_d, gate_b, gw_bd = _prepare_weights(w_in, mlstm_gate_b, lru_gate_w)
    w_out_bf = w_out.astype(BF16)
    head_of = jnp.arange(D_BRANCH) // HEAD_DIM
    ones_bd = (head_of[:, None] == head_of[None, :]).astype(BF16)
    theta = jnp.pad(ret_theta.reshape(DEPTH, 1, 2 * N_HEADS), ((0, 0), (0, 0), (0, LANES - 2 * N_HEADS)))
    rope_tabs = _rope_tables(x_sample.shape[1])

    cvec = jnp.concatenate([c_ctx[None, :], c, jnp.zeros((8 - 1 - bs, D_MODEL), F32)], axis=0)
    mods = _modulation(cvec, w_mod, b_mod).reshape(DEPTH, 8, 3, D_MODEL)

    def layer_weights(l):
        return dict(norm_g=norm_g[l][None, :], w_a=w_a[l], w_b=w_b[l], w_c=w_c[l], w_d=w_d[l],
                    gate_b=gate_b[l], gw_bd=gw_bd[l], lru_conv_w=lru_conv_w[l],
                    lru_conv_b=lru_conv_b[l][None, :], lru_gate_b=lru_gate_b[l], lru_lambda=lru_lambda[l],
                    theta=theta[l], rwkv_mu=rwkv_mu[l][None, :], rwkv_kk=rwkv_kk[l][None, :],
                    rwkv_ka=rwkv_ka[l][None, :], rwkv_rk=rwkv_rk[l][None, :], rwkv_w0=rwkv_w0[l],
                    rwkv_w2=rwkv_w2[l].astype(BF16), rwkv_a0=rwkv_a0[l], rwkv_a2=rwkv_a2[l].astype(BF16),
                    w_out=w_out_bf[l])

    zero_states = (jnp.zeros((bp, 2, N_HEADS, HEAD_DIM, HEAD_DIM), F32),
                   jnp.zeros((bp, 2, N_HEADS, HEAD_DIM), F32),
                   jnp.zeros((bp, 2, N_HEADS), F32),
                   jnp.zeros((bp, 2, D_BRANCH), F32),
                   jnp.zeros((bp, 2, N_HEADS, HEAD_DIM, HEAD_DIM), F32),
                   jnp.zeros((bp, 2, N_HEADS, HEAD_DIM, HEAD_DIM), F32))
    xp = x_prompt
    per_layer = []
    for l in range(DEPTH):
        mod = jnp.broadcast_to(mods[l, 0][None], (bp, 3, D_MODEL))
        xp, st = _layer(xp, mod, layer_weights(l), zero_states, False, None, ones_bd,
                        final_g[None, :] if l == DEPTH - 1 else None)
        per_layer.append(st)
    new_states = tuple(jnp.stack([st[j] for st in per_layer], axis=1) for j in range(6))

    xs = x_sample
    for l in range(DEPTH):
        mod = mods[l, 1:1 + bs]
        st = (state_mlstm_c[:, l], state_mlstm_n[:, l], state_mlstm_m[:, l], state_lru_h[:, l],
              state_ret_r[:, l], state_rwkv_s[:, l])
        xs, _ = _layer(xs, mod, layer_weights(l), st, True, rope_tabs, ones_bd,
                       final_g[None, :] if l == DEPTH - 1 else None)
    return (xp, xs) + new_states
```

```python
import functools

import jax
import jax.numpy as jnp
from jax import lax
from jax.experimental import pallas as pl
from jax.experimental.pallas import tpu as pltpu

F32 = jnp.float32
BF16 = jnp.bfloat16
HIGHEST = lax.Precision.HIGHEST

D_MODEL = 1024
DEPTH = 4
GRID_W = 64
D_BRANCH = 512
HEAD_DIM = 64
N_HEADS = 8
CHUNK = 64
CONV_W = 4
LRU_C = 8.0
LORA = 64
ROPE_BASE = 100.0
EPS = 1e-6
RWKV_SHIFT = 3 * D_BRANCH + 2 * LORA

LANES = 128
TOKEN_BLOCK = 256
INPROJ_BLOCK = 512
VMEM_LIMIT = 48 * 1024 * 1024
NEG = -1e30


def _sigmoid(x):
    return 1.0 / (1.0 + jnp.exp(-x))


def _silu(x):
    return x * _sigmoid(x)


def _softplus(x):
    return jnp.maximum(x, 0.0) + jnp.log(1.0 + jnp.exp(-jnp.abs(x)))


def _log_sigmoid(x):
    return -_softplus(-x)


def _dot(a, b):
    return jnp.dot(a.astype(BF16), b.astype(BF16), preferred_element_type=F32)


def _dot_tn(a, b):
    return lax.dot_general(a.astype(BF16), b.astype(BF16), (((0,), (0,)), ((), ())),
                           preferred_element_type=F32)


def _dot_hi(a, b):
    return jnp.dot(a, b, preferred_element_type=F32, precision=HIGHEST)


_NN = (((1,), (0,)), ((), ()))
_NT = (((1,), (1,)), ((), ()))
_TN = (((0,), (0,)), ((), ()))
_BNN = (((2,), (1,)), ((0,), (0,)))
_BNT = (((2,), (2,)), ((0,), (0,)))
_RWKV_PREC = dict(a=1, inv=3, merge=1, solve=3, rest=1)


def _mm(a, b, dims, passes):
    if passes == 6:
        return lax.dot_general(a, b, dims, preferred_element_type=F32, precision=HIGHEST)
    ah = a.astype(BF16)
    bh = b.astype(BF16)
    out = lax.dot_general(ah, bh, dims, preferred_element_type=F32)
    if passes == 3:
        al = (a - ah.astype(F32)).astype(BF16)
        bl = (b - bh.astype(F32)).astype(BF16)
        out = (out + lax.dot_general(ah, bl, dims, preferred_element_type=F32)
               + lax.dot_general(al, bh, dims, preferred_element_type=F32))
    return out


def _mm_shared(lhs, b, passes):
    n, m = len(lhs), lhs[0].shape[1]
    bh = b.astype(BF16)
    hs = [x.astype(BF16) for x in lhs]
    if passes == 1:
        out = lax.dot_general(jnp.concatenate(hs, axis=1), bh, _BNN, preferred_element_type=F32)
        return [out[:, i * m:(i + 1) * m] for i in range(n)]
    assert passes == 3
    ls = [(x - h.astype(F32)).astype(BF16) for x, h in zip(lhs, hs)]
    bl = (b - bh.astype(F32)).astype(BF16)
    o1 = lax.dot_general(jnp.concatenate(hs + ls, axis=1), bh, _BNN, preferred_element_type=F32)
    o2 = lax.dot_general(jnp.concatenate(hs, axis=1), bl, _BNN, preferred_element_type=F32)
    return [o1[:, i * m:(i + 1) * m] + o1[:, (n + i) * m:(n + i + 1) * m] + o2[:, i * m:(i + 1) * m]
            for i in range(n)]


def _head_sum(x, ones_bd):
    hi = x.astype(BF16)
    lo = (x - hi.astype(F32)).astype(BF16)
    return (jnp.dot(hi, ones_bd, preferred_element_type=F32)
            + jnp.dot(lo, ones_bd, preferred_element_type=F32))


def _head_norm(x, ones_bd):
    return x * lax.rsqrt(_head_sum(x * x, ones_bd) * (1.0 / HEAD_DIM) + EPS)


def _params(*sem):
    return pltpu.CompilerParams(dimension_semantics=sem, vmem_limit_bytes=VMEM_LIMIT)


def _blk_index(reverse, nblk):
    if reverse:
        return lambda b, i: (b, nblk - 1 - i, 0)
    return lambda b, i: (b, i, 0)


def _state_spec(shape):
    nd = len(shape)
    return pl.BlockSpec((1,) + tuple(shape[1:]), lambda b, i: (b,) + (0,) * (nd - 1))


def _const_spec(shape):
    nd = len(shape)
    return pl.BlockSpec(tuple(shape), lambda *_: (0,) * nd)


def _mod_kernel(c_ref, w_ref, b_ref, o_ref):
    sc = _silu(c_ref[...])
    o_ref[0] = _dot(sc, w_ref[0]) + b_ref[0]


def _modulation(cvec, w_mod, b_mod):
    nt = 3
    return pl.pallas_call(
        _mod_kernel,
        out_shape=jax.ShapeDtypeStruct((DEPTH, 8, 3 * D_MODEL), F32),
        grid=(DEPTH, nt),
        in_specs=[pl.BlockSpec((8, D_MODEL), lambda l, j: (0, 0)),
                  pl.BlockSpec((1, D_MODEL, D_MODEL), lambda l, j: (l, 0, j)),
                  pl.BlockSpec((1, 1, D_MODEL), lambda l, j: (l, 0, j))],
        out_specs=pl.BlockSpec((1, 8, D_MODEL), lambda l, j: (l, 0, j)),
        compiler_params=_params("arbitrary", "arbitrary"),
        name="modulation",
    )(cvec, w_mod, b_mod.reshape(DEPTH, 1, 3 * D_MODEL))


def _inproj_kernel(x_ref, g_ref, mod_ref, w_ref, *o_refs, widths):
    x = x_ref[0]
    y = x * lax.rsqrt(jnp.mean(x * x, axis=-1, keepdims=True) + EPS) * g_ref[...]
    h = y * (1.0 + mod_ref[0, 1:2, :]) + mod_ref[0, 0:1, :]
    u = _dot(h, w_ref[...])
    off = 0
    for o_ref, w in zip(o_refs, widths):
        o_ref[0] = u[:, off:off + w]
        off += w


def _inproj(x, g, mod, w, widths):
    b, n, _ = x.shape
    p = w.shape[1]
    tm = INPROJ_BLOCK
    assert n % tm == 0
    return pl.pallas_call(
        functools.partial(_inproj_kernel, widths=widths),
        out_shape=[jax.ShapeDtypeStruct((b, n, wd), F32) for wd in widths],
        grid=(b, n // tm),
        in_specs=[pl.BlockSpec((1, tm, D_MODEL), lambda i, j: (i, j, 0)),
                  _const_spec((1, D_MODEL)),
                  pl.BlockSpec((1, 3, D_MODEL), lambda i, j: (i, 0, 0)),
                  _const_spec((D_MODEL, p))],
        out_specs=[pl.BlockSpec((1, tm, wd), lambda i, j: (i, j, 0)) for wd in widths],
        compiler_params=_params("parallel", "parallel"),
        name="inproj",
    )(x, g, mod, w)


def _expand_exact(x, sel):
    m = x.shape[0]
    hi = x.astype(BF16)
    r1 = x - hi.astype(F32)
    mid = r1.astype(BF16)
    lo = (r1 - mid.astype(F32)).astype(BF16)
    out = jnp.dot(jnp.concatenate([hi, mid, lo], axis=0), sel, preferred_element_type=F32)
    return out[:m] + out[m:2 * m] + out[2 * m:]


def _mlstm_kernel(uf_ref, ub_ref, gb_ref, c0_ref, n0_ref, m0_ref, yf_ref, yb_ref, cf_ref, nf_ref, mf_ref,
                  c_s, n_s, m_s):
    @pl.when(pl.program_id(1) == 0)
    def _():
        _load_blockdiag(c0_ref, c_s)
        n_s[...] = n0_ref[0]
        m_s[...] = m0_ref[0]

    L = CHUNK
    nchunk = TOKEN_BLOCK // L
    npair = D_BRANCH // LANES
    lane = lax.broadcasted_iota(jnp.int32, (L, LANES), 1)
    left = lane < HEAD_DIM
    row_w = lax.broadcasted_iota(jnp.int32, (L, D_BRANCH), 0)
    col_w = lax.broadcasted_iota(jnp.int32, (L, D_BRANCH), 1) & (HEAD_DIM - 1)
    incl_w = (col_w <= row_w, col_w >= row_w)
    diag_w = col_w == row_w
    row_g = lax.broadcasted_iota(jnp.int32, (L, LANES), 0)
    row_l = lax.broadcasted_iota(jnp.int32, (L, L), 0)
    col_l = lax.broadcasted_iota(jnp.int32, (L, L), 1)
    tri = (jnp.where(col_l <= row_l, 1.0, 0.0).astype(F32), jnp.where(col_l >= row_l, 1.0, 0.0).astype(F32))
    sel_r = lax.broadcasted_iota(jnp.int32, (LANES, D_BRANCH), 0)
    sel_h = lax.broadcasted_iota(jnp.int32, (LANES, D_BRANCH), 1) >> 6
    sel_ig = [jnp.where(sel_r == 2 * N_HEADS * d + sel_h, 1.0, 0.0).astype(BF16) for d in range(2)]
    sel_fg = [jnp.where(sel_r == 2 * N_HEADS * d + N_HEADS + sel_h, 1.0, 0.0).astype(BF16) for d in range(2)]
    sq_r = lax.broadcasted_iota(jnp.int32, (LANES, LANES), 0) >> 6
    sq_c = lax.broadcasted_iota(jnp.int32, (LANES, LANES), 1) >> 6
    same_head = sq_r == sq_c
    ones_bd = jnp.where(same_head, 1.0, 0.0).astype(BF16)
    u_refs = (uf_ref, ub_ref)
    scale = HEAD_DIM ** -0.5

    def chunk(ci, carry):
        rows = (pl.ds(pl.multiple_of(ci * L, L), L), pl.ds(pl.multiple_of((nchunk - 1 - ci) * L, L), L))
        parts = []
        for d in range(2):
            u_ref = u_refs[d]
            last = 0 if d == 1 else L - 1
            gates = u_ref[0, rows[d], 3 * D_BRANCH:3 * D_BRANCH + LANES] + gb_ref[...]
            bmat = _dot_hi(tri[d], _log_sigmoid(gates))
            b_at_ig = pltpu.roll(bmat, LANES - N_HEADS, axis=1)
            cmx = gates - b_at_ig
            sh = 1
            while sh < L:
                if d == 0:
                    cmx = jnp.where(row_g >= sh, jnp.maximum(cmx, pltpu.roll(cmx, sh, axis=0)), cmx)
                else:
                    cmx = jnp.where(row_g < L - sh, jnp.maximum(cmx, pltpu.roll(cmx, L - sh, axis=0)), cmx)
                sh *= 2
            rmax = _expand_exact(b_at_ig + cmx, sel_ig[d])
            bcol = _expand_exact(bmat, sel_fg[d])
            igcol = _expand_exact(gates, sel_ig[d])
            crow = jnp.sum(jnp.where(diag_w, igcol - bcol, 0.0), axis=0, keepdims=True)
            m_prev = m_s[d]
            m_inter = bcol + m_prev
            m_t = jnp.maximum(m_inter, rmax)
            pexp = jnp.exp(jnp.where(incl_w[d], bcol + crow, NEG) - m_t)
            w_inter = jnp.exp(m_inter - m_t)
            em = jnp.exp(-m_t)
            b_last = bcol[last:last + 1, :]
            w_log = b_last - bcol + igcol
            m_new = jnp.maximum(b_last + m_prev, jnp.max(w_log, axis=0, keepdims=True))
            dec = jnp.exp(b_last + m_prev - m_new)
            wdec = jnp.exp(w_log - m_new)
            m_s[d] = m_new
            q = u_ref[0, rows[d], 0:D_BRANCH]
            k = u_ref[0, rows[d], D_BRANCH:2 * D_BRANCH] * scale
            v = u_ref[0, rows[d], 2 * D_BRANCH:3 * D_BRANCH]
            parts.append([_pair_split(t) for t in (q, k, v, pexp, w_inter, em, k * wdec, dec)])
        qp, kp, vp, pexp, w_inter, em, wk, dec = (jnp.concatenate(t, axis=0) for t in zip(*parts))
        c_prev = c_s[...]
        n_prev = n_s[...]
        s = _mm(qp, _pair_blockdiag(kp, left), _BNT, 1) * pexp
        num = _mm(s, _pair_blockdiag(vp, left), _BNN, 1) + w_inter * _mm(qp, c_prev, _BNN, 1)
        both = jnp.concatenate([s, qp * n_prev], axis=1).reshape(2 * npair * 2 * L, LANES)
        sums = _head_sum(both, ones_bd).reshape(2 * npair, 2 * L, LANES)
        den = sums[:, :L] + w_inter * sums[:, L:]
        hh = num / jnp.maximum(jnp.abs(den), em)
        n_s[...] = dec * n_prev + jnp.sum(wk, axis=1, keepdims=True)
        for g in range(2 * npair):
            c_s[g] = dec[g] * c_prev[g] + jnp.where(same_head, _dot_tn(wk[g], vp[g]), 0.0)
        for p in range(npair):
            yf_ref[0, rows[0], p * LANES:(p + 1) * LANES] = hh[p].astype(BF16)
            yb_ref[0, rows[1], p * LANES:(p + 1) * LANES] = hh[npair + p].astype(BF16)
        return carry

    lax.fori_loop(0, nchunk, chunk, 0)

    @pl.when(pl.program_id(1) == pl.num_programs(1) - 1)
    def _():
        _store_blockdiag(c_s, cf_ref)
        nf_ref[0] = n_s[...]
        mf_ref[0] = m_s[...]


def _mlstm_scan(u, gate_b, c0, n0, m0):
    b, n, p = u.shape
    nblk = n // TOKEN_BLOCK
    groups = 2 * (D_BRANCH // LANES)
    fwd = lambda bb, i: (bb, i, 0)
    bwd = lambda bb, i: (bb, nblk - 1 - i, 0)
    c0p = _heads_flat(c0)
    n0p = n0.reshape(b, groups, 1, LANES)
    m0p = jnp.repeat(m0, HEAD_DIM, axis=-1)[:, :, None, :]
    states = (c0p, n0p, m0p)
    yf, yb, cfp, nfp, mfp = pl.pallas_call(
        _mlstm_kernel,
        out_shape=[jax.ShapeDtypeStruct((b, n, D_BRANCH), BF16), jax.ShapeDtypeStruct((b, n, D_BRANCH), BF16)]
                  + [jax.ShapeDtypeStruct(s.shape, F32) for s in states],
        grid=(b, nblk),
        in_specs=[pl.BlockSpec((1, TOKEN_BLOCK, p), fwd), pl.BlockSpec((1, TOKEN_BLOCK, p), bwd),
                  _const_spec((1, LANES))] + [_state_spec(s.shape) for s in states],
        out_specs=[pl.BlockSpec((1, TOKEN_BLOCK, D_BRANCH), fwd), pl.BlockSpec((1, TOKEN_BLOCK, D_BRANCH), bwd)]
                  + [_state_spec(s.shape) for s in states],
        scratch_shapes=[pltpu.VMEM((groups, LANES, LANES), F32), pltpu.VMEM(n0p.shape[1:], F32),
                        pltpu.VMEM(m0p.shape[1:], F32)],
        compiler_params=_params("parallel", "arbitrary"),
        name="mlstm_scan",
    )(u, u, gate_b, *states)
    return yf, yb, cfp.reshape(c0.shape), nfp.reshape(n0.shape), mfp[:, :, 0, ::HEAD_DIM]


def _ret_kernel(*refs, rope):
    if rope:
        uf_ref, ub_ref, th_ref, cosf_ref, sinf_ref, cosb_ref, sinb_ref, r0_ref, yf_ref, yb_ref, rf_ref, r_s = refs
        tabs = ((cosf_ref, sinf_ref), (cosb_ref, sinb_ref))
    else:
        uf_ref, ub_ref, th_ref, r0_ref, yf_ref, yb_ref, rf_ref, r_s = refs
    u_refs = (uf_ref, ub_ref)

    @pl.when(pl.program_id(1) == 0)
    def _():
        _load_blockdiag(r0_ref, r_s)

    L = CHUNK
    nchunk = TOKEN_BLOCK // L
    npair = D_BRANCH // LANES
    row = lax.broadcasted_iota(jnp.int32, (L, LANES), 0)
    lane = lax.broadcasted_iota(jnp.int32, (L, LANES), 1)
    col = lane & (HEAD_DIM - 1)
    left = lane < HEAD_DIM
    left_sq = lax.broadcasted_iota(jnp.int32, (LANES, LANES), 1) < HEAD_DIM
    same_head = (lax.broadcasted_iota(jnp.int32, (LANES, LANES), 0) < HEAD_DIM) == left_sq

    def both(fwd, bwd):
        return jnp.concatenate([jnp.broadcast_to(fwd, (npair, L, LANES)),
                                jnp.broadcast_to(bwd, (npair, L, LANES))], axis=0)

    incl = both(col <= row, col >= row)
    diff = jnp.abs(row - col).astype(F32)
    rowf = row.astype(F32)
    p_pos = both(rowf, L - 1.0 - rowf)
    log_g = _log_sigmoid(th_ref[...])
    lg = jnp.stack([jnp.where(left[0:1], log_g[:, N_HEADS * d + 2 * p:N_HEADS * d + 2 * p + 1],
                              log_g[:, N_HEADS * d + 2 * p + 1:N_HEADS * d + 2 * p + 2])
                    for d in range(2) for p in range(npair)], axis=0)
    dmat = jnp.where(incl, jnp.exp(lg * diff), 0.0)
    xi = jnp.exp(lg * (p_pos + 1.0))
    wk = jnp.exp(lg * (L - 1.0 - p_pos))
    dec = jnp.exp(lg * float(L))
    lane_w = lax.broadcasted_iota(jnp.int32, (L, D_BRANCH), 1)
    first_half = (lane_w & (HEAD_DIM - 1)) < (HEAD_DIM // 2)

    def rot(x, c, s):
        swapped = jnp.where(first_half, pltpu.roll(x, D_BRANCH - HEAD_DIM // 2, axis=1),
                            pltpu.roll(x, HEAD_DIM // 2, axis=1))
        return x * c + swapped * s

    def chunk(ci, carry):
        rows = (pl.ds(pl.multiple_of(ci * L, L), L), pl.ds(pl.multiple_of((nchunk - 1 - ci) * L, L), L))
        qs, ks, vs = [], [], []
        for d in range(2):
            q = u_refs[d][0, rows[d], 0:D_BRANCH]
            k = u_refs[d][0, rows[d], D_BRANCH:2 * D_BRANCH]
            if rope:
                c = jnp.concatenate([tabs[d][0][rows[d], :]] * npair, axis=1)
                s = jnp.concatenate([tabs[d][1][rows[d], :]] * npair, axis=1)
                q = rot(q, c, s)
                k = rot(k, c, s)
            qs.append(_pair_split(q))
            ks.append(_pair_split(k * (HEAD_DIM ** -0.5)))
            vs.append(_pair_split(u_refs[d][0, rows[d], 2 * D_BRANCH:3 * D_BRANCH]))
        qp, kp, vp = (jnp.concatenate(t, axis=0) for t in (qs, ks, vs))
        r_prev = r_s[...]
        s_mat = _mm(qp, _pair_blockdiag(kp, left), _BNT, 1) * dmat
        y = _mm(s_mat, _pair_blockdiag(vp, left), _BNN, 1) + xi * _mm(qp, r_prev, _BNN, 1)
        kw = kp * wk
        for g in range(2 * npair):
            r_s[g] = dec[g] * r_prev[g] + jnp.where(same_head, _dot_tn(kw[g], vp[g]), 0.0)
        for p in range(npair):
            yf_ref[0, rows[0], p * LANES:(p + 1) * LANES] = y[p].astype(BF16)
            yb_ref[0, rows[1], p * LANES:(p + 1) * LANES] = y[npair + p].astype(BF16)
        return carry

    lax.fori_loop(0, nchunk, chunk, 0)

    @pl.when(pl.program_id(1) == pl.num_programs(1) - 1)
    def _():
        _store_blockdiag(r_s, rf_ref)


def _load_blockdiag(src_ref, dst_s):
    zero = jnp.zeros((HEAD_DIM, HEAD_DIM), F32)
    for g in range(dst_s.shape[0]):
        top = jnp.concatenate([src_ref[0, 2 * g], zero], axis=1)
        bot = jnp.concatenate([zero, src_ref[0, 2 * g + 1]], axis=1)
        dst_s[g] = jnp.concatenate([top, bot], axis=0)


def _store_blockdiag(src_s, dst_ref):
    for g in range(src_s.shape[0]):
        blk = src_s[g]
        dst_ref[0, 2 * g] = blk[:HEAD_DIM, :HEAD_DIM]
        dst_ref[0, 2 * g + 1] = blk[HEAD_DIM:, HEAD_DIM:]


def _heads_flat(s):
    return s.reshape(s.shape[0], 2 * N_HEADS, HEAD_DIM, HEAD_DIM)


def _ret_scan(u, theta, r0, rope_tabs):
    b, n, p = u.shape
    nblk = n // TOKEN_BLOCK
    fwd = lambda bb, i: (bb, i, 0)
    bwd = lambda bb, i: (bb, nblk - 1 - i, 0)
    rope = rope_tabs is not None
    r0p = _heads_flat(r0)
    in_specs = [pl.BlockSpec((1, TOKEN_BLOCK, p), fwd), pl.BlockSpec((1, TOKEN_BLOCK, p), bwd),
                _const_spec((1, LANES))]
    args = [u, u, theta]
    if rope:
        in_specs += [pl.BlockSpec((TOKEN_BLOCK, LANES), lambda bb, i: (i, 0))] * 2
        in_specs += [pl.BlockSpec((TOKEN_BLOCK, LANES), lambda bb, i: (nblk - 1 - i, 0))] * 2
        args += list(rope_tabs) * 2
    in_specs.append(_state_spec(r0p.shape))
    args.append(r0p)
    yf, yb, rfp = pl.pallas_call(
        functools.partial(_ret_kernel, rope=rope),
        out_shape=[jax.ShapeDtypeStruct((b, n, D_BRANCH), BF16), jax.ShapeDtypeStruct((b, n, D_BRANCH), BF16),
                   jax.ShapeDtypeStruct(r0p.shape, F32)],
        grid=(b, nblk),
        in_specs=in_specs,
        out_specs=[pl.BlockSpec((1, TOKEN_BLOCK, D_BRANCH), fwd), pl.BlockSpec((1, TOKEN_BLOCK, D_BRANCH), bwd),
                   _state_spec(r0p.shape)],
        scratch_shapes=[pltpu.VMEM((2 * (D_BRANCH // LANES), LANES, LANES), F32)],
        compiler_params=_params("parallel", "arbitrary"),
        name="retention_scan",
    )(*args)
    return yf, yb, rfp.reshape(r0.shape)


def _lru_kernel(x_ref, xp_ref, xn_ref, cw_ref, cb_ref, gw_ref, gb_ref, lam_ref, h0_ref,
                y_ref, hf_ref, h_s, *, reverse):
    i = pl.program_id(1)
    nblk = pl.num_programs(1)
    blk = (nblk - 1 - i) if reverse else i

    @pl.when(i == 0)
    def _():
        h_s[...] = h0_ref[0]

    tb = TOKEN_BLOCK
    x = x_ref[0]
    row = lax.broadcasted_iota(jnp.int32, (tb, D_BRANCH), 0)
    prev_ok = jnp.where(blk > 0, 1.0, 0.0)
    next_ok = jnp.where(blk < nblk - 1, 1.0, 0.0)
    p_last = xp_ref[0, 7:8, :] * prev_ok
    n_0 = xn_ref[0, 0:1, :] * next_ok
    n_1 = xn_ref[0, 1:2, :] * next_ok
    xm1 = jnp.where(row == 0, p_last, pltpu.roll(x, 1, axis=0))
    xp1 = jnp.where(row == tb - 1, n_0, pltpu.roll(x, tb - 1, axis=0))
    xp2 = jnp.where(row == tb - 1, n_1, jnp.where(row == tb - 2, n_0, pltpu.roll(x, tb - 2, axis=0)))
    xc = (cw_ref[0:1, :] * xm1 + cw_ref[1:2, :] * x + cw_ref[2:3, :] * xp1 + cw_ref[3:4, :] * xp2
          + cb_ref[...])
    gr = _dot(xc, gw_ref[0]) + gb_ref[0:1, :]
    gi = _dot(xc, gw_ref[1]) + gb_ref[1:2, :]
    log_a = -LRU_C * _sigmoid(gr) * _softplus(-lam_ref[...])
    a = jnp.exp(log_a)
    beta = jnp.sqrt(1.0 - a * a)
    bx = beta * _sigmoid(gi) * xc

    sh = 1
    while sh < tb:
        if sh < 8:
            if reverse:
                ok = row < tb - sh
                a_sh = pltpu.roll(a, tb - sh, axis=0)
                b_sh = pltpu.roll(bx, tb - sh, axis=0)
            else:
                ok = row >= sh
                a_sh = pltpu.roll(a, sh, axis=0)
                b_sh = pltpu.roll(bx, sh, axis=0)
            bx = jnp.where(ok, a * b_sh + bx, bx)
            a = jnp.where(ok, a * a_sh, a)
        elif reverse:
            bx = jnp.concatenate([a[:tb - sh] * bx[sh:] + bx[:tb - sh], bx[tb - sh:]], axis=0)
            a = jnp.concatenate([a[:tb - sh] * a[sh:], a[tb - sh:]], axis=0)
        else:
            bx = jnp.concatenate([bx[:sh], a[sh:] * bx[:tb - sh] + bx[sh:]], axis=0)
            a = jnp.concatenate([a[:sh], a[sh:] * a[:tb - sh]], axis=0)
        sh *= 2
    h = a * h_s[...] + bx
    y_ref[0] = h.astype(BF16)
    h_s[...] = h[0:1, :] if reverse else h[tb - 1:tb, :]

    @pl.when(i == nblk - 1)
    def _():
        hf_ref[0] = h_s[...]


def _lru_scan(xb, conv_w, conv_b, gate_w_bd, gate_b, lam, h0, direction):
    b, n, _ = xb.shape
    tb = TOKEN_BLOCK
    nblk = n // tb
    reverse = direction == 1
    idx = _blk_index(reverse, nblk)
    sub = tb // 8
    nsub = n // 8
    if reverse:
        prev_idx = lambda bb, i: (bb, jnp.maximum((nblk - 1 - i) * sub - 1, 0), 0)
        next_idx = lambda bb, i: (bb, jnp.minimum((nblk - i) * sub, nsub - 1), 0)
    else:
        prev_idx = lambda bb, i: (bb, jnp.maximum(i * sub - 1, 0), 0)
        next_idx = lambda bb, i: (bb, jnp.minimum((i + 1) * sub, nsub - 1), 0)
    return pl.pallas_call(
        functools.partial(_lru_kernel, reverse=reverse),
        out_shape=[jax.ShapeDtypeStruct((b, n, D_BRANCH), BF16), jax.ShapeDtypeStruct(h0.shape, F32)],
        grid=(b, nblk),
        in_specs=[pl.BlockSpec((1, tb, D_BRANCH), idx),
                  pl.BlockSpec((1, 8, D_BRANCH), prev_idx),
                  pl.BlockSpec((1, 8, D_BRANCH), next_idx),
                  _const_spec((CONV_W, D_BRANCH)), _const_spec((1, D_BRANCH)),
                  _const_spec((2, D_BRANCH, D_BRANCH)), _const_spec((2, D_BRANCH)),
                  _const_spec((1, D_BRANCH)), _state_spec(h0.shape)],
        out_specs=[pl.BlockSpec((1, tb, D_BRANCH), idx), _state_spec(h0.shape)],
        scratch_shapes=[pltpu.VMEM((1, D_BRANCH), F32)],
        compiler_params=_params("parallel", "arbitrary"),
        name="rglru_scan",
    )(xb, xb, xb, conv_w, conv_b, gate_w_bd, gate_b, lam, h0)


def _rwkv_prep_kernel(s_ref, sp_ref, sn_ref, mu_ref, kk_ref, ka_ref, rk_ref, w0_ref, w2_ref,
                      a0_ref, a2_ref, ones_ref,
                      r_ref, v_ref, kkn_ref, bonus_ref, ld_ref, kt_ref, bb_ref, *, grid_shift):
    tb = TOKEN_BLOCK
    s = s_ref[0]
    width = s.shape[1]
    row = lax.broadcasted_iota(jnp.int32, (tb, width), 0)
    lane = lax.broadcasted_iota(jnp.int32, (tb, width), 1)
    if grid_shift:
        i = pl.program_id(1)
        nblk = pl.num_programs(1)
        col = row & (GRID_W - 1)
        qc = width // 4
        up_halo = sp_ref[0] * jnp.where(i > 0, 1.0, 0.0)
        dn_halo = sn_ref[0] * jnp.where(i < nblk - 1, 1.0, 0.0)
        left = jnp.where(col == 0, 0.0, pltpu.roll(s, 1, axis=0))
        right = jnp.where(col == GRID_W - 1, 0.0, pltpu.roll(s, tb - 1, axis=0))
        up = jnp.concatenate([up_halo, s[:tb - GRID_W]], axis=0)
        down = jnp.concatenate([s[GRID_W:], dn_halo], axis=0)
        sh = jnp.where(lane < qc, left,
                       jnp.where(lane < 2 * qc, right, jnp.where(lane < 3 * qc, up, down)))
    else:
        prev = jnp.where(row == 0, 0.0, pltpu.roll(s, 1, axis=0))
        nxt = jnp.where(row == tb - 1, 0.0, pltpu.roll(s, tb - 1, axis=0))
        sh = jnp.where(lane < width // 2, prev, nxt)
    s = s + mu_ref[...] * (sh - s)
    r = s[:, 0:D_BRANCH]
    k = s[:, D_BRANCH:2 * D_BRANCH]
    v = s[:, 2 * D_BRANCH:3 * D_BRANCH]
    lw = jnp.tanh(s[:, 3 * D_BRANCH:3 * D_BRANCH + LORA])
    la = s[:, 3 * D_BRANCH + LORA:3 * D_BRANCH + 2 * LORA]
    ones_bd = ones_ref[...]
    kkh = k * kk_ref[...]
    kk = kkh / jnp.maximum(jnp.sqrt(_head_sum(kkh * kkh, ones_bd)), 1e-12)
    r_ref[0] = r
    v_ref[0] = v
    kkn_ref[0] = kk
    bonus_ref[0] = _head_sum(r * k * rk_ref[...], ones_bd) * v
    for d in range(2):
        w_log = -_softplus(-(w0_ref[d:d + 1, :] + _dot(lw, w2_ref[d]))) - 0.5
        ld_ref[d, 0] = -jnp.exp(w_log)
        a = _sigmoid(a0_ref[d:d + 1, :] + _dot(la, a2_ref[d]))
        kt_ref[d, 0] = k * (1.0 + (a - 1.0) * ka_ref[...])
        bb_ref[d, 0] = kk * a


def _rwkv_prep(s, mu, k_k, k_a, r_k, w0, w2, a0, a2, ones_bd, grid_shift):
    b, n, width = s.shape
    tb = TOKEN_BLOCK
    nblk = n // tb
    assert grid_shift or nblk == 1
    halo = GRID_W
    sub = tb // halo
    nsub = n // halo
    tok = lambda bb, i: (bb, i, 0)
    tok2 = lambda bb, i: (0, bb, i, 0)
    out1 = jax.ShapeDtypeStruct((b, n, D_BRANCH), F32)
    out2 = jax.ShapeDtypeStruct((2, b, n, D_BRANCH), F32)
    return pl.pallas_call(
        functools.partial(_rwkv_prep_kernel, grid_shift=grid_shift),
        out_shape=[out1, out1, out1, out1, out2, out2, out2],
        grid=(b, nblk),
        in_specs=[pl.BlockSpec((1, tb, width), tok),
                  pl.BlockSpec((1, halo, width), lambda bb, i: (bb, jnp.maximum(i * sub - 1, 0), 0)),
                  pl.BlockSpec((1, halo, width), lambda bb, i: (bb, jnp.minimum((i + 1) * sub, nsub - 1), 0)),
                  _const_spec((1, width)), _const_spec((1, D_BRANCH)), _const_spec((1, D_BRANCH)),
                  _const_spec((1, D_BRANCH)), _const_spec((2, D_BRANCH)),
                  _const_spec((2, LORA, D_BRANCH)), _const_spec((2, D_BRANCH)),
                  _const_spec((2, LORA, D_BRANCH)), _const_spec((D_BRANCH, D_BRANCH))],
        out_specs=[pl.BlockSpec((1, tb, D_BRANCH), tok)] * 4
                  + [pl.BlockSpec((2, 1, tb, D_BRANCH), tok2)] * 3,
        compiler_params=_params("parallel", "parallel"),
        name="rwkv_prep",
    )(s, s, s, mu, k_k, k_a, r_k, w0, w2, a0, a2, ones_bd)


def _pair_split(x):
    return jnp.stack([x[:, p * LANES:(p + 1) * LANES] for p in range(D_BRANCH // LANES)], axis=0)


def _pair_blockdiag(p, left):
    return jnp.concatenate([jnp.where(left, p, 0.0), jnp.where(left, 0.0, p)], axis=1)


def _rwkv_kernel(rf_ref, vf_ref, kkf_ref, ldf_ref, ktf_ref, bbf_ref,
                 rb_ref, vb_ref, kkb_ref, ldb_ref, ktb_ref, bbb_ref, s0_ref,
                 yf_ref, yb_ref, sf_ref, s_s):
    @pl.when(pl.program_id(1) == 0)
    def _():
        for g in range(s_s.shape[0]):
            s_s[g] = jnp.concatenate([s0_ref[0, 2 * g], s0_ref[0, 2 * g + 1]], axis=1)

    L = CHUNK
    nchunk = TOKEN_BLOCK // L
    npair = D_BRANCH // LANES
    row = lax.broadcasted_iota(jnp.int32, (L, LANES), 0)
    lane = lax.broadcasted_iota(jnp.int32, (L, LANES), 1)
    col = lane & (HEAD_DIM - 1)
    left = lane < HEAD_DIM

    def both(fwd, bwd):
        return jnp.concatenate([jnp.broadcast_to(fwd, (npair, L, LANES)),
                                jnp.broadcast_to(bwd, (npair, L, LANES))], axis=0)

    incl = both(col <= row, col >= row)
    strict = both(col < row, col > row)
    eye_p = jnp.where(col == row, 1.0, 0.0).astype(F32)
    same16 = (row >> 4) == (col >> 4)
    same32 = (row >> 5) == (col >> 5)
    off32 = jnp.logical_and(same32, jnp.logical_not(same16))
    off64 = jnp.logical_not(same32)
    row_l = lax.broadcasted_iota(jnp.int32, (L, L), 0)
    col_l = lax.broadcasted_iota(jnp.int32, (L, L), 1)
    tri = (jnp.where(col_l <= row_l, 1.0, 0.0).astype(F32), jnp.where(col_l >= row_l, 1.0, 0.0).astype(F32))
    dir_refs = ((rf_ref, vf_ref, kkf_ref, ldf_ref, ktf_ref, bbf_ref),
                (rb_ref, vb_ref, kkb_ref, ldb_ref, ktb_ref, bbb_ref))
    pa, pi, pm, ps, pr = (_RWKV_PREC[k] for k in ('a', 'inv', 'merge', 'solve', 'rest'))

    def pair_mm(xs, p, passes):
        return _mm_shared(xs, _pair_blockdiag(p, left), passes)

    def chunk(ci, carry):
        rows = (pl.ds(pl.multiple_of(ci * L, L), L), pl.ds(pl.multiple_of((nchunk - 1 - ci) * L, L), L))
        kap, bet, kti, rti, vp, g_last = [], [], [], [], [], []
        for d in range(2):
            r_ref, v_ref, kk_ref, ld_ref, kt_ref, bb_ref = dir_refs[d]
            ld = ld_ref[0, 0, rows[d], :]
            cs = _dot_hi(tri[d], ld)
            g_in = jnp.exp(cs)
            g_inv = jnp.exp(-cs)
            kap.append(_pair_split(kk_ref[0, rows[d], :] * jnp.exp(cs - ld)))
            bet.append(_pair_split(bb_ref[0, 0, rows[d], :] * g_inv))
            kti.append(_pair_split(kt_ref[0, 0, rows[d], :] * g_inv))
            rti.append(_pair_split(r_ref[0, rows[d], :] * g_in))
            vp.append(_pair_split(v_ref[0, rows[d], :]))
            last = 0 if d == 1 else L - 1
            g_last.append(_pair_split(g_in[last:last + 1, :]))
        kap, bet, kti, rti, vp, g_last = (jnp.concatenate(t, axis=0) for t in (kap, bet, kti, rti, vp, g_last))
        s_prev = s_s[...]
        x2 = jnp.concatenate([kap, rti], axis=1)
        a_b = _mm(x2, _pair_blockdiag(bet, left), _BNT, pa)
        a_k = _mm(x2, _pair_blockdiag(kti, left), _BNT, pa)
        p_s = _mm(x2, _pair_blockdiag(s_prev, left), _BNT, pr)
        a_ub = jnp.where(strict, a_b[:, :L], 0.0)
        a_uk = jnp.where(strict, a_k[:, :L], 0.0)
        a_rb = jnp.where(incl, a_b[:, L:], 0.0)
        a_rk = jnp.where(incl, a_k[:, L:], 0.0)
        nil = jnp.where(same16, -a_ub, 0.0)
        tinv = eye_p + nil
        (pw,) = pair_mm([nil], nil, pi)
        for _ in range(2):
            t_pw, pw2 = pair_mm([tinv, pw], pw, pi)
            tinv, pw = tinv + t_pw, pw2
        tinv = tinv + pair_mm([tinv], pw, pi)[0]
        for off in (off32, off64):
            (a_t,) = pair_mm([jnp.where(off, a_ub, 0.0)], tinv, pm)
            tinv = tinv - pair_mm([tinv], a_t, pm)[0]
        (a_v,) = pair_mm([a_uk], vp, pr)
        uh = -pair_mm([tinv], p_s[:, :L] + a_v, ps)[0]
        a_r = jnp.concatenate([a_rb, a_rk], axis=2)
        uv_bd = jnp.concatenate([_pair_blockdiag(uh, left), _pair_blockdiag(vp, left)], axis=1)
        y = p_s[:, L:] + _mm(a_r, uv_bd, _BNN, pr)
        uv = jnp.concatenate([uh, vp], axis=1)
        bk = jnp.concatenate([bet, kti], axis=1)
        for g in range(2 * npair):
            upd = _mm(uv[g], bk[g], _TN, pr)
            s_s[g] = (s_prev[g] + jnp.where(left, upd[:HEAD_DIM], upd[HEAD_DIM:])) * g_last[g]
        for p in range(npair):
            yf_ref[0, rows[0], p * LANES:(p + 1) * LANES] = y[p].astype(BF16)
            yb_ref[0, rows[1], p * LANES:(p + 1) * LANES] = y[npair + p].astype(BF16)
        return carry

    lax.fori_loop(0, nchunk, chunk, 0)

    @pl.when(pl.program_id(1) == pl.num_programs(1) - 1)
    def _():
        for g in range(s_s.shape[0]):
            pair = s_s[g]
            sf_ref[0, 2 * g] = pair[:, :HEAD_DIM]
            sf_ref[0, 2 * g + 1] = pair[:, HEAD_DIM:]


def _rwkv_scan(r, v, kk, ld, kt, bb, s0):
    b, n, _ = r.shape
    nblk = n // TOKEN_BLOCK
    npair = D_BRANCH // LANES
    s0p = _heads_flat(s0)
    fwd = lambda bb_, i: (bb_, i, 0)
    bwd = lambda bb_, i: (bb_, nblk - 1 - i, 0)
    fwd2 = lambda bb_, i: (0, bb_, i, 0)
    bwd2 = lambda bb_, i: (1, bb_, nblk - 1 - i, 0)
    tok = lambda im: pl.BlockSpec((1, TOKEN_BLOCK, D_BRANCH), im)
    tok2 = lambda im: pl.BlockSpec((1, 1, TOKEN_BLOCK, D_BRANCH), im)
    yf, yb, sfp = pl.pallas_call(
        _rwkv_kernel,
        out_shape=[jax.ShapeDtypeStruct((b, n, D_BRANCH), BF16), jax.ShapeDtypeStruct((b, n, D_BRANCH), BF16),
                   jax.ShapeDtypeStruct(s0p.shape, F32)],
        grid=(b, nblk),
        in_specs=[tok(fwd), tok(fwd), tok(fwd), tok2(fwd2), tok2(fwd2), tok2(fwd2),
                  tok(bwd), tok(bwd), tok(bwd), tok2(bwd2), tok2(bwd2), tok2(bwd2), _state_spec(s0p.shape)],
        out_specs=[tok(fwd), tok(bwd), _state_spec(s0p.shape)],
        scratch_shapes=[pltpu.VMEM((2 * npair, HEAD_DIM, LANES), F32)],
        compiler_params=_params("parallel", "arbitrary"),
        name="rwkv_scan",
    )(r, v, kk, ld, kt, bb, r, v, kk, ld, kt, bb, s0p)
    return yf, yb, sfp.reshape(s0.shape)


def _combine_kernel(x_ref, mod_ref, maf_ref, mab_ref, og_ref, lf_ref, lb_ref, lg_ref,
                    rf_ref, rb_ref, rg_ref, wf_ref, wb_ref, bonus_ref, wg_ref, ones_ref, wo_ref,
                    *rest, final):
    if final:
        fg_ref, o_ref = rest
    else:
        (o_ref,) = rest
    ones_bd = ones_ref[...]
    og = og_ref[0]
    def both(f_ref, b_ref):
        return f_ref[0].astype(F32) + b_ref[0].astype(F32)

    h_a = both(maf_ref, mab_ref) * _sigmoid(og[:, :D_BRANCH])
    y_a = _head_norm(h_a, ones_bd) * _silu(og[:, D_BRANCH:])
    y_b = both(lf_ref, lb_ref) * _silu(lg_ref[0])
    y_c = _head_norm(both(rf_ref, rb_ref), ones_bd) * _silu(rg_ref[0])
    y_d = (_head_norm(both(wf_ref, wb_ref), ones_bd) + bonus_ref[0]) * _silu(wg_ref[0])
    y = (_dot(y_a, wo_ref[0:D_BRANCH, :]) + _dot(y_b, wo_ref[D_BRANCH:2 * D_BRANCH, :])
         + _dot(y_c, wo_ref[2 * D_BRANCH:3 * D_BRANCH, :]) + _dot(y_d, wo_ref[3 * D_BRANCH:, :]))
    x = x_ref[0] + mod_ref[0, 2:3, :] * y
    if final:
        x = x * lax.rsqrt(jnp.mean(x * x, axis=-1, keepdims=True) + EPS) * fg_ref[...]
    o_ref[0] = x


def _combine(x, mod, branch_arrays, ones_bd, w_out, final_g):
    b, n, _ = x.shape
    tm = TOKEN_BLOCK
    tok = lambda i, j: (i, j, 0)
    final = final_g is not None
    in_specs = [pl.BlockSpec((1, tm, D_MODEL), tok), pl.BlockSpec((1, 3, D_MODEL), lambda i, j: (i, 0, 0))]
    in_specs += [pl.BlockSpec((1, tm, a.shape[-1]), tok) for a in branch_arrays]
    in_specs += [_const_spec((D_BRANCH, D_BRANCH)), _const_spec(w_out.shape)]
    args = [x, mod, *branch_arrays, ones_bd, w_out]
    if final:
        in_specs.append(_const_spec((1, D_MODEL)))
        args.append(final_g)
    return pl.pallas_call(
        functools.partial(_combine_kernel, final=final),
        out_shape=jax.ShapeDtypeStruct(x.shape, F32),
        grid=(b, n // tm),
        in_specs=in_specs,
        out_specs=pl.BlockSpec((1, tm, D_MODEL), tok),
        compiler_params=_params("parallel", "parallel"),
        name="combine_outproj",
    )(*args)


def _prepare_weights(w_in, mlstm_gate_b, lru_gate_w):
    p_mlstm = 5 * D_BRANCH + 4 * N_HEADS
    a0, a1 = 0, p_mlstm
    b1 = a1 + 2 * D_BRANCH
    c1 = b1 + 4 * D_BRANCH
    wa = w_in[:, :, a0:a1]
    pad = jnp.zeros((DEPTH, D_MODEL, LANES - 4 * N_HEADS), w_in.dtype)
    w_a = jnp.concatenate([wa[:, :, :3 * D_BRANCH], wa[:, :, 5 * D_BRANCH:], pad,
                           wa[:, :, 3 * D_BRANCH:5 * D_BRANCH]], axis=-1).astype(BF16)
    w_b = w_in[:, :, a1:b1].astype(BF16)
    w_c = w_in[:, :, b1:c1].astype(BF16)
    w_d = w_in[:, :, c1:].astype(BF16)
    gate_b = jnp.pad(mlstm_gate_b, ((0, 0), (0, LANES - 4 * N_HEADS)))[:, None, :]
    eye_h = jnp.eye(N_HEADS, dtype=lru_gate_w.dtype)
    gw_bd = jnp.einsum('ldghij,hk->ldghikj', lru_gate_w, eye_h).reshape(
        DEPTH, 2, 2, D_BRANCH, D_BRANCH).astype(BF16)
    return w_a, w_b, w_c, w_d, gate_b, gw_bd


def _rope_tables(n):
    rows = n // GRID_W
    row_idx = jnp.broadcast_to(jnp.arange(rows, dtype=F32)[:, None], (rows, GRID_W)).reshape(-1)
    col_idx = jnp.broadcast_to(jnp.arange(GRID_W, dtype=F32)[None, :], (rows, GRID_W)).reshape(-1)
    n_freq = HEAD_DIM // 4
    freqs = ROPE_BASE ** (-jnp.arange(n_freq, dtype=F32) / n_freq)
    ang = jnp.concatenate([row_idx[:, None] * freqs, col_idx[:, None] * freqs], -1)
    cos, sin = jnp.cos(ang), jnp.sin(ang)
    cos_t = jnp.concatenate([cos, cos] * (LANES // HEAD_DIM), axis=-1)
    sin_t = jnp.concatenate([-sin, sin] * (LANES // HEAD_DIM), axis=-1)
    return cos_t, sin_t


def _layer(x, mod, lw, states, latent, rope_tabs, ones_bd, final_g):
    c0, n0, m0, h0, r0, s0 = states
    norm_g = lw['norm_g']
    bsz, n, _ = x.shape
    x_in, mod_in = (x, mod) if latent else (x.reshape(1, bsz * n, D_MODEL), mod[:1])

    def project(w, widths):
        return [u.reshape(bsz, n, u.shape[-1]) for u in _inproj(x_in, norm_g, mod_in, w, widths)]

    u_a, og_a = project(lw['w_a'], (3 * D_BRANCH + LANES, 2 * D_BRANCH))
    x_b, g_b = project(lw['w_b'], (D_BRANCH, D_BRANCH))
    u_c, g_c = project(lw['w_c'], (3 * D_BRANCH, D_BRANCH))
    s_d, g_d = project(lw['w_d'], (RWKV_SHIFT, D_BRANCH))

    prep = _rwkv_prep(s_d, lw['rwkv_mu'], lw['rwkv_kk'], lw['rwkv_ka'], lw['rwkv_rk'], lw['rwkv_w0'],
                      lw['rwkv_w2'], lw['rwkv_a0'], lw['rwkv_a2'], ones_bd, latent)
    r_d, v_d, kk_d, bonus_d, ld_d, kt_d, bb_d = prep

    yaf, yab, c_new, n_new, m_new = _mlstm_scan(u_a, lw['gate_b'], c0, n0, m0)
    lru = [_lru_scan(x_b, lw['lru_conv_w'], lw['lru_conv_b'], lw['gw_bd'][d], lw['lru_gate_b'][d],
                     lw['lru_lambda'][d][None, :], h0[:, d][:, None, :], d) for d in range(2)]
    (ybf, hf), (ybb, hb) = lru
    ycf, ycb, r_new = _ret_scan(u_c, lw['theta'], r0, rope_tabs)
    ydf, ydb, s_new = _rwkv_scan(r_d, v_d, kk_d, ld_d, kt_d, bb_d, s0)
    branch_arrays = [yaf, yab, og_a, ybf, ybb, g_b, ycf, ycb, g_c, ydf, ydb, bonus_d, g_d]
    x_new = _combine(x, mod, branch_arrays, ones_bd, lw['w_out'], final_g)
    h_new = jnp.stack([hf[:, 0, :], hb[:, 0, :]], axis=1)
    return x_new, (c_new, n_new, m_new, h_new, r_new, s_new)


def kernel(x_prompt, x_sample, c, state_mlstm_c, state_mlstm_n, state_mlstm_m, state_lru_h, state_ret_r,
           state_rwkv_s, c_ctx, norm_g, w_mod, b_mod, w_in, w_out, mlstm_gate_b, lru_conv_w, lru_conv_b,
           lru_gate_w, lru_gate_b, lru_lambda, ret_theta, rwkv_mu, rwkv_w0, rwkv_w2, rwkv_a0, rwkv_a2,
           rwkv_kk, rwkv_ka, rwkv_rk, final_g):
    bp = x_prompt.shape[0]
    bs = x_sample.shape[0]
    assert x_prompt.shape[1] == TOKEN_BLOCK and x_sample.shape[1] % TOKEN_BLOCK == 0
    assert 1 + bs <= 8

    w_a, w_b, w_c, w_d, gate_b, gw_bd = _prepare_weights(w_in, mlstm_gate_b, lru_gate_w)
    w_out_bf = w_out.astype(BF16)
    head_of = jnp.arange(D_BRANCH) // HEAD_DIM
    ones_bd = (head_of[:, None] == head_of[None, :]).astype(BF16)
    theta = jnp.pad(ret_theta.reshape(DEPTH, 1, 2 * N_HEADS), ((0, 0), (0, 0), (0, LANES - 2 * N_HEADS)))
    rope_tabs = _rope_tables(x_sample.shape[1])

    cvec = jnp.concatenate([c_ctx[None, :], c, jnp.zeros((8 - 1 - bs, D_MODEL), F32)], axis=0)
    mods = _modulation(cvec, w_mod, b_mod).reshape(DEPTH, 8, 3, D_MODEL)

    def layer_weights(l):
        return dict(norm_g=norm_g[l][None, :], w_a=w_a[l], w_b=w_b[l], w_c=w_c[l], w_d=w_d[l],
                    gate_b=gate_b[l], gw_bd=gw_bd[l], lru_conv_w=lru_conv_w[l],
                    lru_conv_b=lru_conv_b[l][None, :], lru_gate_b=lru_gate_b[l], lru_lambda=lru_lambda[l],
                    theta=theta[l], rwkv_mu=rwkv_mu[l][None, :], rwkv_kk=rwkv_kk[l][None, :],
                    rwkv_ka=rwkv_ka[l][None, :], rwkv_rk=rwkv_rk[l][None, :], rwkv_w0=rwkv_w0[l],
                    rwkv_w2=rwkv_w2[l].astype(BF16), rwkv_a0=rwkv_a0[l], rwkv_a2=rwkv_a2[l].astype(BF16),
                    w_out=w_out_bf[l])

    zero_states = (jnp.zeros((bp, 2, N_HEADS, HEAD_DIM, HEAD_DIM), F32),
                   jnp.zeros((bp, 2, N_HEADS, HEAD_DIM), F32),
                   jnp.zeros((bp, 2, N_HEADS), F32),
                   jnp.zeros((bp, 2, D_BRANCH), F32),
                   jnp.zeros((bp, 2, N_HEADS, HEAD_DIM, HEAD_DIM), F32),
                   jnp.zeros((bp, 2, N_HEADS, HEAD_DIM, HEAD_DIM), F32))
    xp = x_prompt
    per_layer = []
    for l in range(DEPTH):
        mod = jnp.broadcast_to(mods[l, 0][None], (bp, 3, D_MODEL))
        xp, st = _layer(xp, mod, layer_weights(l), zero_states, False, None, ones_bd,
                        final_g[None, :] if l == DEPTH - 1 else None)
        per_layer.append(st)
    new_states = tuple(jnp.stack([st[j] for st in per_layer], axis=1) for j in range(6))

    xs = x_sample
    for l in range(DEPTH):
        mod = mods[l, 1:1 + bs]
        st = (state_mlstm_c[:, l], state_mlstm_n[:, l], state_mlstm_m[:, l], state_lru_h[:, l],
              state_ret_r[:, l], state_rwkv_s[:, l])
        xs, _ = _layer(xs, mod, layer_weights(l), st, True, rope_tabs, ones_bd,
                       final_g[None, :] if l == DEPTH - 1 else None)
    return (xp, xs) + new_states
```
